```python
import math
import jax
import jax.numpy as jnp
from jax import lax
import numpy as np


D_MODEL = 2048
BATCH = 4
SEQ = 2048
DEPTH = 2

GRID_W = 64
CTX_LEN = 256
HEAD_DIM = 128
NA_HEADS = 8
NA_WIDTH = NA_HEADS * HEAD_DIM
NA_KH = 8
NA_KW = 16
GQA_HEADS = 8
GQA_KV_HEADS = 2
GQA_GROUP = GQA_HEADS // GQA_KV_HEADS
GQA_Q_WIDTH = GQA_HEADS * HEAD_DIM
GQA_KV_WIDTH = GQA_KV_HEADS * HEAD_DIM
Q_BLOCK = 128
ROPE_THETA = 10000.0
S5_WIDTH = 1024
S5_GROUP_CH = 16
S5_GROUPS = S5_WIDTH // S5_GROUP_CH
S5_STATE = 64
N_EXPERTS = 32
TOP_K = 4
D_FF_EXPERT = 1024
SWIGLU_ALPHA = 1.702
SWIGLU_LIMIT = 7.0
MOE_BLOCK = 128
N_BRANCHES = 3
RMS_EPS = 1e-6
CTX_SPLITS = (NA_WIDTH, NA_WIDTH, GQA_KV_WIDTH, GQA_KV_WIDTH, S5_WIDTH)
QUERY_SPLITS = (NA_WIDTH, GQA_Q_WIDTH, D_MODEL, D_MODEL, D_MODEL)
IN_SPLITS = CTX_SPLITS + QUERY_SPLITS
CTX_WIDTH = 2 * NA_WIDTH + 2 * GQA_KV_WIDTH + S5_WIDTH
IN_WIDTH = CTX_WIDTH + NA_WIDTH + GQA_Q_WIDTH + N_BRANCHES * D_MODEL

kernel_name = 'hybrid_na_s5_gqa_moe_diffusion_block'


def rms_norm(x, g):
    xf = x.astype(jnp.float32)
    y = xf * lax.rsqrt(jnp.mean(xf * xf, axis=-1, keepdims=True) + RMS_EPS)
    return (y * g.astype(jnp.float32)).astype(x.dtype)


def split_cols(z, widths):
    offs = [int(o) for o in np.cumsum(widths)[:-1]]
    return jnp.split(z, offs, axis=-1)


def axial_rope_tables(seq_len):
    t = jnp.arange(seq_len, dtype=jnp.int32)
    row = (t // GRID_W).astype(jnp.float32)
    col = (t % GRID_W).astype(jnp.float32)
    n_freq = HEAD_DIM // 4
    inv_freq = ROPE_THETA ** (-jnp.arange(n_freq, dtype=jnp.float32) / n_freq)
    ang = jnp.concatenate([row[:, None] * inv_freq, col[:, None] * inv_freq], axis=-1)
    return jnp.cos(ang), jnp.sin(ang)


def apply_axial_rope(x, cos, sin):
    nf = HEAD_DIM // 4
    xf = x.astype(jnp.float32)
    c = cos[None, :, None, :]
    s = sin[None, :, None, :]

    def rot(xa, ca, sa):
        x1, x2 = xa[..., :nf], xa[..., nf:]
        return jnp.concatenate([x1 * ca - x2 * sa, x2 * ca + x1 * sa], axis=-1)

    out = jnp.concatenate([rot(xf[..., :HEAD_DIM // 2], c[..., :nf], s[..., :nf]),
                           rot(xf[..., HEAD_DIM // 2:], c[..., nf:], s[..., nf:])], axis=-1)
    return out.astype(x.dtype)


def dense_attention(q, k, v):
    s = jnp.einsum('bqhgd,bkhd->bhgqk', q, k, preferred_element_type=jnp.float32) * (q.shape[-1] ** -0.5)
    p = jax.nn.softmax(s, axis=-1).astype(v.dtype)
    return jnp.einsum('bhgqk,bkhd->bqhgd', p, v)


def blockwise_attention(q, k, v):
    B, S, Hk, G, hd = q.shape
    nb = S // Q_BLOCK
    qb = q.reshape(B, nb, Q_BLOCK, Hk, G, hd).transpose(1, 0, 2, 3, 4, 5)
    ob = lax.map(lambda qi: dense_attention(qi, k, v), qb)
    return ob.transpose(1, 0, 2, 3, 4, 5).reshape(B, S, Hk * G * hd)


def neighbourhood_attention(q, k, v, k_ctx, v_ctx, rel_bias):
    B, S, H, hd = q.shape
    rows = S // GRID_W
    kh = min(NA_KH, rows)
    kw = NA_KW
    r = jnp.arange(rows, dtype=jnp.int32)
    cidx = jnp.arange(GRID_W, dtype=jnp.int32)
    row_start = jnp.clip(r - kh // 2, 0, rows - kh)
    row_idx = row_start[:, None] + jnp.arange(kh, dtype=jnp.int32)[None, :]
    col_start = jnp.clip(cidx - kw // 2, 0, GRID_W - kw)
    col_ok = (cidx[None, :] >= col_start[:, None]) & (cidx[None, :] < col_start[:, None] + kw)
    dr_idx = row_idx - r[:, None] + (NA_KH - 1)
    dc_idx = jnp.clip(cidx[None, :] - cidx[:, None] + (NA_KW - 1), 0, 2 * NA_KW - 2)
    bias = rel_bias.astype(jnp.float32)[:, dr_idx[:, None, :, None], dc_idx[None, :, None, :]]
    qg = q.reshape(B, rows, GRID_W, H, hd)
    kb = k.reshape(B, rows, GRID_W, H, hd)[:, row_idx]
    vb = v.reshape(B, rows, GRID_W, H, hd)[:, row_idx]
    scale = hd ** -0.5
    s_loc = jnp.einsum('brqhd,brikhd->bhrqik', qg, kb, preferred_element_type=jnp.float32) * scale + bias[None]
    s_loc = jnp.where(col_ok[:, None, :], s_loc, -jnp.inf)
    s_ctx = jnp.einsum('brqhd,blhd->bhrql', qg, k_ctx, preferred_element_type=jnp.float32) * scale
    n_loc = kh * GRID_W
    p = jax.nn.softmax(jnp.concatenate([s_loc.reshape(B, H, rows, GRID_W, n_loc), s_ctx], axis=-1), axis=-1)
    p = p.astype(v.dtype)
    p_loc = p[..., :n_loc].reshape(B, H, rows, GRID_W, kh, GRID_W)
    out = (jnp.einsum('bhrqik,brikhd->brqhd', p_loc, vb)
           + jnp.einsum('bhrql,blhd->brqhd', p[..., n_loc:], v_ctx))
    return out.reshape(B, S, H * hd)


def s5_discretize(lam_re, lam_im, log_dt, b_re, b_im):
    lam = lax.complex(lam_re.astype(jnp.float32), lam_im.astype(jnp.float32))
    dt = jnp.exp(log_dt.astype(jnp.float32))[..., None]
    lam_bar = jnp.exp(lam * dt)
    bmat = lax.complex(b_re.astype(jnp.float32), b_im.astype(jnp.float32))
    b_bar = ((lam_bar - 1.0) / lam)[..., None] * bmat
    return lam_bar, b_bar


def s5_scan(u, lam_bar, b_bar, h0, reverse):
    bu = jnp.einsum('btgp,gnp->btgn', u.astype(jnp.complex64), b_bar)
    if h0 is not None:
        edge = -1 if reverse else 0
        bu = bu.at[:, edge].add(lam_bar * h0)
    a = jnp.broadcast_to(lam_bar, bu.shape)

    def combine(e1, e2):
        a1, b1 = e1
        a2, b2 = e2
        return a1 * a2, a2 * b1 + b2

    _, states = lax.associative_scan(combine, (a, bu), axis=1, reverse=reverse)
    return states


def s5_readout(states, c_re, c_im):
    return (jnp.einsum('btgn,gpn->btgp', states.real, c_re.astype(jnp.float32))
            - jnp.einsum('btgn,gpn->btgp', states.imag, c_im.astype(jnp.float32)))


def s5_glu(y, w_glu):
    g = jax.nn.gelu(y)
    return g * jax.nn.sigmoid(g @ w_glu.astype(jnp.float32))


def s5_branch(u, u_c, lam_bar, b_bar, c_re, c_im, d_skip, w_glu, ctx_out):
    B, S, _ = u.shape
    L = u_c.shape[1]
    ug = u.astype(jnp.float32).reshape(B, S, S5_GROUPS, S5_GROUP_CH)
    ug_c = u_c.astype(jnp.float32).reshape(B, L, S5_GROUPS, S5_GROUP_CH)
    d = d_skip.astype(jnp.float32)
    y = u.astype(jnp.float32) * d
    y_c = u_c.astype(jnp.float32) * d if ctx_out else None
    for direction in range(2):
        reverse = direction == 1
        st_c = s5_scan(ug_c, lam_bar[direction], b_bar[direction], None, reverse)
        h_ctx = st_c[:, 0] if reverse else st_c[:, -1]
        st = s5_scan(ug, lam_bar[direction], b_bar[direction], h_ctx, reverse)
        y = y + s5_readout(st, c_re[direction], c_im[direction]).reshape(B, S, S5_WIDTH)
        if ctx_out:
            y_c = y_c + s5_readout(st_c, c_re[direction], c_im[direction]).reshape(B, L, S5_WIDTH)
    out = s5_glu(y, w_glu).astype(u.dtype)
    out_c = s5_glu(y_c, w_glu).astype(u.dtype) if ctx_out else None
    return out, out_c


def gated_merge(y_a, y_s, y_c, g_a, g_s, g_c, w_br_a, w_br_s, w_br_c, w_out):
    m = (jax.nn.sigmoid(g_a) * (y_a @ w_br_a)
         + jax.nn.sigmoid(g_s) * (y_s @ w_br_s)
         + jax.nn.sigmoid(g_c) * (y_c @ w_br_c))
    return m @ w_out


def hybrid_mixer(h, hc, w_in, rel_bias, q_norm_g, k_norm_g, lam_bar, b_bar, c_re, c_im, d_skip, w_glu,
                 w_br_a, w_br_s, w_br_c, w_out, cos, sin, ctx_out):
    B, S, _ = h.shape
    L = hc.shape[1]
    hd = HEAD_DIM
    ka, va, kg, vg, u, qa, qg, ga, gs, gc = split_cols(h @ w_in, IN_SPLITS)
    if ctx_out:
        parts_c = split_cols(hc @ w_in, IN_SPLITS)
    else:
        parts_c = split_cols(hc @ w_in[:, :CTX_WIDTH], CTX_SPLITS)
    ka_c, va_c, kg_c, vg_c, u_c = parts_c[:5]
    ka_c = ka_c.reshape(B, L, NA_HEADS, hd)
    va_c = va_c.reshape(B, L, NA_HEADS, hd)
    y_a = neighbourhood_attention(qa.reshape(B, S, NA_HEADS, hd), ka.reshape(B, S, NA_HEADS, hd),
                                  va.reshape(B, S, NA_HEADS, hd), ka_c, va_c, rel_bias)
    q = apply_axial_rope(rms_norm(qg.reshape(B, S, GQA_HEADS, hd), q_norm_g), cos, sin)
    k = apply_axial_rope(rms_norm(kg.reshape(B, S, GQA_KV_HEADS, hd), k_norm_g), cos, sin)
    k_c = rms_norm(kg_c.reshape(B, L, GQA_KV_HEADS, hd), k_norm_g)
    v_c = vg_c.reshape(B, L, GQA_KV_HEADS, hd)
    k_all = jnp.concatenate([k, k_c], axis=1)
    v_all = jnp.concatenate([vg.reshape(B, S, GQA_KV_HEADS, hd), v_c], axis=1)
    y_c = blockwise_attention(q.reshape(B, S, GQA_KV_HEADS, GQA_GROUP, hd), k_all, v_all)
    y_s, y_s_c = s5_branch(u, u_c, lam_bar, b_bar, c_re, c_im, d_skip, w_glu, ctx_out)
    out = gated_merge(y_a, y_s, y_c, ga, gs, gc, w_br_a, w_br_s, w_br_c, w_out)
    if not ctx_out:
        return out, None
    qa_c, qg_c, ga_c, gs_c, gc_c = parts_c[5:]
    y_a_c = dense_attention(qa_c.reshape(B, L, NA_HEADS, 1, hd), ka_c, va_c).reshape(B, L, NA_WIDTH)
    q_c = rms_norm(qg_c.reshape(B, L, GQA_HEADS, hd), q_norm_g)
    y_c_c = dense_attention(q_c.reshape(B, L, GQA_KV_HEADS, GQA_GROUP, hd), k_c, v_c).reshape(B, L, GQA_Q_WIDTH)
    out_c = gated_merge(y_a_c, y_s_c, y_c_c, ga_c, gs_c, gc_c, w_br_a, w_br_s, w_br_c, w_out)
    return out, out_c


def clamped_swiglu(a):
    gate_in = jnp.minimum(a[..., ::2], SWIGLU_LIMIT)
    lin = jnp.clip(a[..., 1::2], -SWIGLU_LIMIT, SWIGLU_LIMIT)
    return gate_in * jax.nn.sigmoid(SWIGLU_ALPHA * gate_in) * (lin + 1.0)


def moe_ffn(h, w_router, b_router, w1, b1, w2, b2):
    T, D = h.shape
    n_assign = T * TOP_K
    logits = jnp.einsum('td,de->te', h, w_router, preferred_element_type=jnp.float32) + b_router.astype(jnp.float32)
    top_val, top_idx = lax.top_k(logits, TOP_K)
    weights = jax.nn.softmax(top_val, axis=-1)
    flat_e = top_idx.reshape(-1)
    order = jnp.argsort(flat_e)
    sorted_e = flat_e[order]
    token = order // TOP_K
    w_sorted = weights.reshape(-1)[order]
    counts = jnp.zeros((N_EXPERTS,), jnp.int32).at[flat_e].add(1)
    starts = jnp.cumsum(counts) - counts
    padded = (counts + MOE_BLOCK - 1) // MOE_BLOCK * MOE_BLOCK
    padded_end = jnp.cumsum(padded)
    padded_start = padded_end - padded
    dest = padded_start[sorted_e] + jnp.arange(n_assign, dtype=jnp.int32) - starts[sorted_e]
    n_blocks = -(-n_assign // MOE_BLOCK) + N_EXPERTS
    buf = jnp.zeros((n_blocks * MOE_BLOCK, D), h.dtype).at[dest].set(h[token])
    block_start = jnp.arange(n_blocks, dtype=jnp.int32) * MOE_BLOCK
    block_expert = jnp.minimum(jnp.searchsorted(padded_end, block_start, side='right'), N_EXPERTS - 1)

    def expert_block(args):
        xb, e = args
        a = xb @ w1[e] + b1[e]
        return clamped_swiglu(a) @ w2[e] + b2[e]

    y = lax.map(expert_block, (buf.reshape(n_blocks, MOE_BLOCK, D), block_expert))
    y = y.reshape(n_blocks * MOE_BLOCK, D)[dest]
    out = jnp.zeros((T, D), jnp.float32).at[token].add(y.astype(jnp.float32) * w_sorted[:, None])
    return out.astype(h.dtype)


def setup_inputs(seed: int = 0) -> dict:
    key = jax.random.key(seed)
    ks = jax.random.split(key, 32)
    f32 = jnp.float32
    D = D_MODEL
    G, N, P = S5_GROUPS, S5_STATE, S5_GROUP_CH

    def nrm(k, shape, scale):
        return jax.random.normal(k, shape, f32) * scale

    n_idx = jnp.arange(N, dtype=f32)
    return {
        'x': nrm(ks[0], (BATCH, SEQ, D), 1.0),
        'c': nrm(ks[1], (BATCH, D), 1.0),
        'ctx': nrm(ks[2], (BATCH, CTX_LEN, D), 1.0),
        'c_ctx': nrm(ks[3], (D,), 1.0),
        'norm1_g': 1.0 + nrm(ks[4], (DEPTH, D), 0.02),
        'norm2_g': 1.0 + nrm(ks[5], (DEPTH, D), 0.02),
        'w_ada': nrm(ks[6], (DEPTH, D, 6 * D), 0.5 * D ** -0.5),
        'b_ada': nrm(ks[7], (DEPTH, 6 * D), 0.02),
        'w_in': nrm(ks[8], (DEPTH, D, IN_WIDTH), D ** -0.5),
        'na_rel_bias': nrm(ks[9], (DEPTH, NA_HEADS, 2 * NA_KH - 1, 2 * NA_KW - 1), 0.1),
        'q_norm_g': 1.0 + nrm(ks[10], (DEPTH, HEAD_DIM), 0.02),
        'k_norm_g': 1.0 + nrm(ks[11], (DEPTH, HEAD_DIM), 0.02),
        's5_lam_re': -0.5 + nrm(ks[12], (DEPTH, 2, G, N), 0.01),
        's5_lam_im': math.pi * n_idx + nrm(ks[13], (DEPTH, 2, G, N), 0.01),
        's5_log_dt': jax.random.uniform(ks[14], (DEPTH, 2, G), f32, math.log(1e-3), math.log(1e-1)),
        's5_b_re': nrm(ks[15], (DEPTH, 2, G, N, P), (2 * P) ** -0.5),
        's5_b_im': nrm(ks[16], (DEPTH, 2, G, N, P), (2 * P) ** -0.5),
        's5_c_re': nrm(ks[17], (DEPTH, 2, G, P, N), (2 * N) ** -0.5),
        's5_c_im': nrm(ks[18], (DEPTH, 2, G, P, N), (2 * N) ** -0.5),
        's5_d': nrm(ks[19], (DEPTH, S5_WIDTH), 1.0),
        's5_w_glu': nrm(ks[20], (DEPTH, S5_WIDTH, S5_WIDTH), S5_WIDTH ** -0.5),
        'w_br_a': nrm(ks[21], (DEPTH, NA_WIDTH, D), NA_WIDTH ** -0.5),
        'w_br_s': nrm(ks[22], (DEPTH, S5_WIDTH, D), S5_WIDTH ** -0.5),
        'w_br_c': nrm(ks[23], (DEPTH, GQA_Q_WIDTH, D), GQA_Q_WIDTH ** -0.5),
        'w_out': nrm(ks[24], (DEPTH, D, D), D ** -0.5),
        'w_router': nrm(ks[25], (DEPTH, D, N_EXPERTS), D ** -0.5),
        'b_router': nrm(ks[26], (DEPTH, N_EXPERTS), 0.01),
        'w_exp1': nrm(ks[27], (DEPTH, N_EXPERTS, D, 2 * D_FF_EXPERT), D ** -0.5),
        'b_exp1': nrm(ks[28], (DEPTH, N_EXPERTS, 2 * D_FF_EXPERT), 0.01),
        'w_exp2': nrm(ks[29], (DEPTH, N_EXPERTS, D_FF_EXPERT, D), D_FF_EXPERT ** -0.5),
        'b_exp2': nrm(ks[30], (DEPTH, N_EXPERTS, D), 0.01),
        'final_norm_g': 1.0 + nrm(ks[31], (D,), 0.02),
    }


def reference(x, c, ctx, c_ctx, norm1_g, norm2_g, w_ada, b_ada, w_in, na_rel_bias, q_norm_g, k_norm_g,
              s5_lam_re, s5_lam_im, s5_log_dt, s5_b_re, s5_b_im, s5_c_re, s5_c_im, s5_d, s5_w_glu,
              w_br_a, w_br_s, w_br_c, w_out, w_router, b_router, w_exp1, b_exp1, w_exp2, b_exp2,
              final_norm_g):
    B, S, D = x.shape
    L = ctx.shape[1]
    cos, sin = axial_rope_tables(S)
    silu_c = jax.nn.silu(c)
    silu_cc = jax.nn.silu(c_ctx)
    xc = ctx
    for l in range(DEPTH):
        ctx_out = l < DEPTH - 1
        mod = (silu_c @ w_ada[l] + b_ada[l])[:, None, :]
        modc = silu_cc @ w_ada[l] + b_ada[l]
        sh1, sc1, g1, sh2, sc2, g2 = jnp.split(mod, 6, axis=-1)
        sh1c, sc1c, g1c, sh2c, sc2c, g2c = jnp.split(modc, 6, axis=-1)
        h = rms_norm(x, norm1_g[l]) * (1.0 + sc1) + sh1
        hc = rms_norm(xc, norm1_g[l]) * (1.0 + sc1c) + sh1c
        lam_bar, b_bar = s5_discretize(s5_lam_re[l], s5_lam_im[l], s5_log_dt[l], s5_b_re[l], s5_b_im[l])
        y, yc = hybrid_mixer(h, hc, w_in[l], na_rel_bias[l], q_norm_g[l], k_norm_g[l], lam_bar, b_bar,
                             s5_c_re[l], s5_c_im[l], s5_d[l], s5_w_glu[l], w_br_a[l], w_br_s[l], w_br_c[l],
                             w_out[l], cos, sin, ctx_out)
        x = x + g1 * y
        h = rms_norm(x, norm2_g[l]) * (1.0 + sc2) + sh2
        if ctx_out:
            xc = xc + g1c * yc
            hc = rms_norm(xc, norm2_g[l]) * (1.0 + sc2c) + sh2c
            f = moe_ffn(jnp.concatenate([h.reshape(B * S, D), hc.reshape(B * L, D)], axis=0),
                        w_router[l], b_router[l], w_exp1[l], b_exp1[l], w_exp2[l], b_exp2[l])
            x = x + g2 * f[:B * S].reshape(B, S, D)
            xc = xc + g2c * f[B * S:].reshape(B, L, D)
        else:
            f = moe_ffn(h.reshape(B * S, D), w_router[l], b_router[l], w_exp1[l], b_exp1[l], w_exp2[l], b_exp2[l])
            x = x + g2 * f.reshape(B, S, D)
    return rms_norm(x, final_norm_g)
```

```python
import functools
import math

import jax
import jax.numpy as jnp
from jax import lax
from jax.experimental import pallas as pl
from jax.experimental.pallas import tpu as pltpu

F32 = jnp.float32
BF16 = jnp.bfloat16
I32 = jnp.int32

D_MODEL = 2048
BATCH = 4
SEQ = 2048
DEPTH = 2
GRID_W = 64
GRID_H = SEQ // GRID_W
CTX_LEN = 256
HEAD_DIM = 128
NA_HEADS = 8
NA_WIDTH = NA_HEADS * HEAD_DIM
NA_KH = 8
NA_KW = 16
GQA_HEADS = 8
GQA_KV_HEADS = 2
GQA_GROUP = GQA_HEADS // GQA_KV_HEADS
GQA_Q_WIDTH = GQA_HEADS * HEAD_DIM
GQA_KV_WIDTH = GQA_KV_HEADS * HEAD_DIM
ROPE_THETA = 10000.0
S5_WIDTH = 1024
S5_GROUP_CH = 16
S5_GROUPS = S5_WIDTH // S5_GROUP_CH
S5_STATE = 64
N_EXPERTS = 32
TOP_K = 4
D_FF_EXPERT = 1024
SWIGLU_ALPHA = 1.702
SWIGLU_LIMIT = 7.0
RMS_EPS = 1e-6

N_LAT = BATCH * SEQ
N_CTX = BATCH * CTX_LEN
N_ALL = N_LAT + N_CTX
CTX_WIDTH = 2 * NA_WIDTH + 2 * GQA_KV_WIDTH + S5_WIDTH
IN_WIDTH = CTX_WIDTH + NA_WIDTH + GQA_Q_WIDTH + 3 * D_MODEL
COL_KA = 0
COL_VA = NA_WIDTH
COL_KG = 2 * NA_WIDTH
COL_VG = COL_KG + GQA_KV_WIDTH
COL_U = COL_VG + GQA_KV_WIDTH
COL_QA = CTX_WIDTH
COL_QG = COL_QA + NA_WIDTH
COL_GA = COL_QG + GQA_Q_WIDTH
COL_GS = COL_GA + D_MODEL
COL_GC = COL_GS + D_MODEL

LANES = 128
ROW_CHUNKS = D_MODEL // LANES
NEG = -1e30
ATT_SCALE = HEAD_DIM ** -0.5

S5_CHUNK = 16
S5_SEQ = CTX_LEN + SEQ
S5_NCHUNK = S5_SEQ // S5_CHUNK
S5_CTX_CHUNKS = CTX_LEN // S5_CHUNK
S5_ROWS = S5_NCHUNK * BATCH
S5_UNITS = S5_GROUPS // 2
S5_LANES = S5_GROUPS * S5_STATE

MOE_BLOCK = 256
VMEM_LIMIT = 56 * 1024 * 1024


def _cparams(*sem):
    return pltpu.CompilerParams(dimension_semantics=sem, vmem_limit_bytes=VMEM_LIMIT)


def _sigmoid(x):
    return 1.0 / (1.0 + jnp.exp(-x))


def _batch_of_block(i, rows_per_block):
    return jnp.minimum(i // (SEQ // rows_per_block), BATCH)


def _ada_kernel(c_ref, w_ref, b_ref, o_ref):
    c = c_ref[...]
    a = (c * _sigmoid(c)).astype(BF16)
    o_ref[...] = jnp.dot(a, w_ref[...].astype(BF16), preferred_element_type=F32) + b_ref[...]


def _ada_mod(cc, w_ada, b_ada):
    tn = 1024
    return pl.pallas_call(
        _ada_kernel,
        grid=(DEPTH, 6 * D_MODEL // tn),
        in_specs=[
            pl.BlockSpec((8, D_MODEL), lambda l, j: (0, 0)),
            pl.BlockSpec((None, D_MODEL, tn), lambda l, j: (l, 0, j)),
            pl.BlockSpec((None, 1, tn), lambda l, j: (l, 0, j)),
        ],
        out_specs=pl.BlockSpec((None, 8, tn), lambda l, j: (l, 0, j)),
        out_shape=jax.ShapeDtypeStruct((DEPTH, 8, 6 * D_MODEL), F32),
        compiler_params=_cparams("parallel", "parallel"),
    )(cc, w_ada, b_ada.reshape(DEPTH, 1, 6 * D_MODEL))


def _normmod_kernel(x_ref, g_ref, sc_ref, sh_ref, o_ref):
    x = x_ref[...]
    y = x * lax.rsqrt(jnp.mean(x * x, axis=-1, keepdims=True) + RMS_EPS) * g_ref[...]
    o_ref[...] = (y * (1.0 + sc_ref[...]) + sh_ref[...]).astype(o_ref.dtype)


def _norm_mod(x, g, mod3, shift_idx, scale_idx):
    rows = x.shape[0]
    tm = 256
    return pl.pallas_call(
        _normmod_kernel,
        grid=(rows // tm,),
        in_specs=[
            pl.BlockSpec((tm, D_MODEL), lambda i: (i, 0)),
            pl.BlockSpec((1, D_MODEL), lambda i: (0, 0)),
            pl.BlockSpec((None, 1, D_MODEL), lambda i: (_batch_of_block(i, tm), 0, scale_idx)),
            pl.BlockSpec((None, 1, D_MODEL), lambda i: (_batch_of_block(i, tm), 0, shift_idx)),
        ],
        out_specs=pl.BlockSpec((tm, D_MODEL), lambda i: (i, 0)),
        out_shape=jax.ShapeDtypeStruct((rows, D_MODEL), BF16),
        compiler_params=_cparams("parallel"),
    )(x, g.reshape(1, D_MODEL), mod3, mod3)


def _final_norm_kernel(x_ref, g_ref, o_ref):
    x = x_ref[...]
    o_ref[...] = x * lax.rsqrt(jnp.mean(x * x, axis=-1, keepdims=True) + RMS_EPS) * g_ref[...]


def _final_norm(x, g):
    rows = x.shape[0]
    tm = 256
    return pl.pallas_call(
        _final_norm_kernel,
        grid=(rows // tm,),
        in_specs=[pl.BlockSpec((tm, D_MODEL), lambda i: (i, 0)),
                  pl.BlockSpec((1, D_MODEL), lambda i: (0, 0))],
        out_specs=pl.BlockSpec((tm, D_MODEL), lambda i: (i, 0)),
        out_shape=jax.ShapeDtypeStruct((rows, D_MODEL), F32),
        compiler_params=_cparams("parallel"),
    )(x, g.reshape(1, D_MODEL))


def _mm_kernel(a_ref, w_ref, o_ref):
    o_ref[...] = jnp.dot(a_ref[...], w_ref[...].astype(BF16),
                         preferred_element_type=F32).astype(o_ref.dtype)


def _matmul(a, w, out_dtype, tm=1024, tn=512):
    m, k = a.shape
    n = w.shape[1]
    return pl.pallas_call(
        _mm_kernel,
        grid=(m // tm, n // tn),
        in_specs=[pl.BlockSpec((tm, k), lambda i, j: (i, 0)),
                  pl.BlockSpec((k, tn), lambda i, j: (0, j))],
        out_specs=pl.BlockSpec((tm, tn), lambda i, j: (i, j)),
        out_shape=jax.ShapeDtypeStruct((m, n), out_dtype),
        compiler_params=_cparams("parallel", "parallel"),
    )(a, w)


NA_QROWS = 4
NA_BAND = NA_KH + NA_QROWS
NA_TQ = NA_QROWS * GRID_W
NA_TK = NA_BAND * GRID_W


def _na_bias_table(rel_bias):
    cidx = jnp.arange(GRID_W, dtype=I32)
    col_start = jnp.clip(cidx - NA_KW // 2, 0, GRID_W - NA_KW)
    col_ok = (cidx[None, :] >= col_start[:, None]) & (cidx[None, :] < col_start[:, None] + NA_KW)
    dc = jnp.clip(cidx[None, :] - cidx[:, None] + (NA_KW - 1), 0, 2 * NA_KW - 2)
    t = rel_bias.astype(F32)[:, :, dc]
    t = jnp.where(col_ok[None, None], t, NEG)
    pad = jnp.full((NA_HEADS, 1, GRID_W, GRID_W), NEG, F32)
    t = jnp.concatenate([pad, t, pad], axis=1)
    return jnp.concatenate([t[:, :-1], t[:, 1:]], axis=-1)


def _na_kernel(q_ref, k_ref, v_ref, kc_ref, vc_ref, pt_ref, o_ref):
    i = pl.program_id(2)
    start = jnp.clip(NA_QROWS * i - NA_KH // 2, 0, GRID_H - NA_BAND)
    koff = pl.multiple_of(start * GRID_W, GRID_W * NA_QROWS)
    q = q_ref[...].astype(BF16)
    kb = k_ref[pl.ds(koff, NA_TK), :].astype(BF16)
    vb = v_ref[pl.ds(koff, NA_TK), :].astype(BF16)
    nt = (((1,), (1,)), ((), ()))
    s = lax.dot_general(q, kb, nt, preferred_element_type=F32) * ATT_SCALE
    lane = lax.broadcasted_iota(I32, (GRID_W, 2 * GRID_W), 1)
    bias_rows = []
    for a in range(NA_QROWS):
        rq = NA_QROWS * i + a
        ws = jnp.clip(rq - NA_KH // 2, 0, GRID_H - NA_KH)
        tiles = []
        for jp in range(NA_BAND // 2):
            rk = start + 2 * jp
            e = jnp.clip(rk - rq + NA_KH, 0, 2 * NA_KH - 1)
            ok0 = ((rk >= ws) & (rk < ws + NA_KH)).astype(I32)
            ok1 = ((rk + 1 >= ws) & (rk + 1 < ws + NA_KH)).astype(I32)
            ok = jnp.where(lane < GRID_W, ok0, ok1) > 0
            tiles.append(jnp.where(ok, pt_ref[e], NEG))
        bias_rows.append(jnp.concatenate(tiles, axis=1))
    s = s + jnp.concatenate(bias_rows, axis=0)
    sc = lax.dot_general(q, kc_ref[...].astype(BF16), nt, preferred_element_type=F32) * ATT_SCALE
    m = jnp.maximum(jnp.max(s, axis=-1, keepdims=True), jnp.max(sc, axis=-1, keepdims=True))
    p = jnp.exp(s - m)
    pc = jnp.exp(sc - m)
    den = jnp.sum(p, axis=-1, keepdims=True) + jnp.sum(pc, axis=-1, keepdims=True)
    o = (jnp.dot(p.astype(BF16), vb, preferred_element_type=F32)
         + jnp.dot(pc.astype(BF16), vc_ref[...].astype(BF16), preferred_element_type=F32))
    o_ref[...] = (o / den).astype(o_ref.dtype)


def _na_attention(z, pt):
    qb = SEQ // NA_TQ
    cq, ck, cv = COL_QA // HEAD_DIM, COL_KA // HEAD_DIM, COL_VA // HEAD_DIM
    ctx_blk = N_LAT // CTX_LEN
    return pl.pallas_call(
        _na_kernel,
        grid=(BATCH, NA_HEADS, qb),
        in_specs=[
            pl.BlockSpec((NA_TQ, HEAD_DIM), lambda b, h, i: (b * qb + i, cq + h)),
            pl.BlockSpec((SEQ, HEAD_DIM), lambda b, h, i: (b, ck + h)),
            pl.BlockSpec((SEQ, HEAD_DIM), lambda b, h, i: (b, cv + h)),
            pl.BlockSpec((CTX_LEN, HEAD_DIM), lambda b, h, i: (ctx_blk + b, ck + h)),
            pl.BlockSpec((CTX_LEN, HEAD_DIM), lambda b, h, i: (ctx_blk + b, cv + h)),
            pl.BlockSpec((None, 2 * NA_KH, GRID_W, 2 * GRID_W), lambda b, h, i: (h, 0, 0, 0)),
        ],
        out_specs=pl.BlockSpec((NA_TQ, HEAD_DIM), lambda b, h, i: (b * qb + i, h)),
        out_shape=jax.ShapeDtypeStruct((N_LAT, NA_WIDTH), BF16),
        compiler_params=_cparams("parallel", "parallel", "parallel"),
    )(z, z, z, z, z, pt)


def _softmax_attend(q, ks, vs):
    nt = (((1,), (1,)), ((), ()))
    ss = [lax.dot_general(q, k, nt, preferred_element_type=F32) * ATT_SCALE for k in ks]
    m = ss[0].max(axis=-1, keepdims=True)
    for s in ss[1:]:
        m = jnp.maximum(m, s.max(axis=-1, keepdims=True))
    ps = [jnp.exp(s - m) for s in ss]
    den = ps[0].sum(axis=-1, keepdims=True)
    for p in ps[1:]:
        den = den + p.sum(axis=-1, keepdims=True)
    o = jnp.dot(ps[0].astype(BF16), vs[0], preferred_element_type=F32)
    for p, v in zip(ps[1:], vs[1:]):
        o = o + jnp.dot(p.astype(BF16), v, preferred_element_type=F32)
    return o / den


def _stack_heads(q, group):
    return jnp.concatenate([q[:, g * HEAD_DIM:(g + 1) * HEAD_DIM] for g in range(group)], axis=0)


def _unstack_heads(o, group, rows):
    return jnp.concatenate([o[g * rows:(g + 1) * rows] for g in range(group)], axis=1)


def _ctx_attn_kernel(q_ref, k_ref, v_ref, o_ref, *, group):
    q = _stack_heads(q_ref[...].astype(BF16), group)
    o = _softmax_attend(q, [k_ref[...].astype(BF16)], [v_ref[...].astype(BF16)])
    o_ref[...] = _unstack_heads(o, group, CTX_LEN).astype(o_ref.dtype)


def _ctx_attention(q_arr, q_col, k_arr, k_col, v_arr, v_col, kv_heads, group):
    ctx_blk = N_LAT // CTX_LEN
    qw = group * HEAD_DIM
    return pl.pallas_call(
        functools.partial(_ctx_attn_kernel, group=group),
        grid=(BATCH, kv_heads),
        in_specs=[
            pl.BlockSpec((CTX_LEN, qw), lambda b, h: (ctx_blk + b, q_col // group + h)),
            pl.BlockSpec((CTX_LEN, HEAD_DIM), lambda b, h: (ctx_blk + b, k_col + h)),
            pl.BlockSpec((CTX_LEN, HEAD_DIM), lambda b, h: (ctx_blk + b, v_col + h)),
        ],
        out_specs=pl.BlockSpec((CTX_LEN, qw), lambda b, h: (b, h)),
        out_shape=jax.ShapeDtypeStruct((N_CTX, kv_heads * qw), BF16),
        compiler_params=_cparams("parallel", "parallel"),
    )(q_arr, k_arr, v_arr)


def _rope_tables():
    t = jnp.arange(SEQ, dtype=I32)
    row = (t // GRID_W).astype(F32)
    col = (t % GRID_W).astype(F32)
    n_freq = HEAD_DIM // 4
    inv_freq = ROPE_THETA ** (-jnp.arange(n_freq, dtype=F32) / n_freq)
    ar = row[:, None] * inv_freq
    ac = col[:, None] * inv_freq
    cos = jnp.concatenate([jnp.cos(ar), jnp.cos(ar), jnp.cos(ac), jnp.cos(ac)], axis=-1)
    sin = jnp.concatenate([-jnp.sin(ar), jnp.sin(ar), -jnp.sin(ac), jnp.sin(ac)], axis=-1)
    cos = jnp.concatenate([cos, jnp.ones((CTX_LEN, HEAD_DIM), F32)], axis=0)
    sin = jnp.concatenate([sin, jnp.zeros((CTX_LEN, HEAD_DIM), F32)], axis=0)
    return cos, sin


def _qkprep_kernel(q0_ref, q1_ref, k_ref, cos_ref, sin_ref, qg_ref, kg_ref, o_ref):
    c = cos_ref[...]
    s = sin_ref[...]
    lane = lax.broadcasted_iota(I32, c.shape, 1)
    first = (lane % (HEAD_DIM // 2)) < (HEAD_DIM // 4)

    def prep(x, g):
        y = x * lax.rsqrt(jnp.mean(x * x, axis=-1, keepdims=True) + RMS_EPS) * g
        partner = jnp.where(first, pltpu.roll(y, HEAD_DIM - HEAD_DIM // 4, 1),
                            pltpu.roll(y, HEAD_DIM // 4, 1))
        return (y * c + partner * s).astype(o_ref.dtype)

    half = GQA_HEADS // 2
    for h in range(half):
        sl = slice(h * HEAD_DIM, (h + 1) * HEAD_DIM)
        o_ref[:, sl] = prep(q0_ref[:, sl], qg_ref[...])
        o_ref[:, (half + h) * HEAD_DIM:(half + h + 1) * HEAD_DIM] = prep(q1_ref[:, sl], qg_ref[...])
    for h in range(GQA_KV_HEADS):
        sl = slice(h * HEAD_DIM, (h + 1) * HEAD_DIM)
        o_ref[:, (GQA_HEADS + h) * HEAD_DIM:(GQA_HEADS + h + 1) * HEAD_DIM] = prep(k_ref[:, sl], kg_ref[...])


def _qk_prep(z, q_norm_g, k_norm_g, cos, sin):
    tm = 256
    hq = GQA_Q_WIDTH // 2
    lat_blocks = N_LAT // tm
    per_batch = SEQ // tm
    tbl = lambda i: (jnp.where(i < lat_blocks, i % per_batch, per_batch), 0)
    return pl.pallas_call(
        _qkprep_kernel,
        grid=(N_ALL // tm,),
        in_specs=[
            pl.BlockSpec((tm, hq), lambda i: (i, COL_QG // hq)),
            pl.BlockSpec((tm, hq), lambda i: (i, COL_QG // hq + 1)),
            pl.BlockSpec((tm, GQA_KV_WIDTH), lambda i: (i, COL_KG // GQA_KV_WIDTH)),
            pl.BlockSpec((tm, HEAD_DIM), tbl),
            pl.BlockSpec((tm, HEAD_DIM), tbl),
            pl.BlockSpec((1, HEAD_DIM), lambda i: (0, 0)),
            pl.BlockSpec((1, HEAD_DIM), lambda i: (0, 0)),
        ],
        out_specs=pl.BlockSpec((tm, GQA_Q_WIDTH + GQA_KV_WIDTH), lambda i: (i, 0)),
        out_shape=jax.ShapeDtypeStruct((N_ALL, GQA_Q_WIDTH + GQA_KV_WIDTH), BF16),
        compiler_params=_cparams("parallel"),
    )(z, z, z, cos, sin, q_norm_g.reshape(1, HEAD_DIM), k_norm_g.reshape(1, HEAD_DIM))


GQA_TQ = 256


def _gqa_kernel(q_ref, k_ref, kc_ref, v_ref, vc_ref, o_ref):
    q = _stack_heads(q_ref[...], GQA_GROUP)
    o = _softmax_attend(q, [k_ref[...], kc_ref[...]],
                        [v_ref[...].astype(BF16), vc_ref[...].astype(BF16)])
    o_ref[...] = _unstack_heads(o, GQA_GROUP, GQA_TQ).astype(o_ref.dtype)


def _gqa_attention(qk, z):
    qb = SEQ // GQA_TQ
    qw = GQA_GROUP * HEAD_DIM
    kcol = GQA_Q_WIDTH // HEAD_DIM
    vcol = COL_VG // HEAD_DIM
    ctx_blk = N_LAT // CTX_LEN
    return pl.pallas_call(
        _gqa_kernel,
        grid=(BATCH, GQA_KV_HEADS, qb),
        in_specs=[
            pl.BlockSpec((GQA_TQ, qw), lambda b, h, i: (b * qb + i, h)),
            pl.BlockSpec((SEQ, HEAD_DIM), lambda b, h, i: (b, kcol + h)),
            pl.BlockSpec((CTX_LEN, HEAD_DIM), lambda b, h, i: (ctx_blk + b, kcol + h)),
            pl.BlockSpec((SEQ, HEAD_DIM), lambda b, h, i: (b, vcol + h)),
            pl.BlockSpec((CTX_LEN, HEAD_DIM), lambda b, h, i: (ctx_blk + b, vcol + h)),
        ],
        out_specs=pl.BlockSpec((GQA_TQ, qw), lambda b, h, i: (b * qb + i, h)),
        out_shape=jax.ShapeDtypeStruct((N_LAT, GQA_Q_WIDTH), BF16),
        compiler_params=_cparams("parallel", "parallel", "parallel"),
    )(qk, qk, qk, z, z)


def _pair_blockdiag(x):
    g, a, b = x.shape
    x2 = x.reshape(g // 2, 2, a, b)
    eye = jnp.eye(2, dtype=x.dtype)
    return jnp.einsum('qiab,ij->qiajb', x2, eye).reshape(g // 2, 2 * a, 2 * b)


def _s5_weights(lam_re, lam_im, log_dt, b_re, b_im, c_re, c_im, d_skip):
    hi = lax.Precision.HIGHEST
    G, N, P, C = S5_GROUPS, S5_STATE, S5_GROUP_CH, S5_CHUNK
    lam_re, lam_im = lam_re.astype(F32), lam_im.astype(F32)
    dt = jnp.exp(log_dt.astype(F32))[..., None]
    tau = jnp.arange(C + 1, dtype=F32)[None, None, :, None]
    mag = jnp.exp(lam_re[:, :, None, :] * dt[:, :, None, :] * tau)
    ang = lam_im[:, :, None, :] * dt[:, :, None, :] * tau
    lp_re, lp_im = mag * jnp.cos(ang), mag * jnp.sin(ang)
    nr, ni = lp_re[:, :, 1] - 1.0, lp_im[:, :, 1]
    den = lam_re * lam_re + lam_im * lam_im
    fr = (nr * lam_re + ni * lam_im) / den
    fi = (ni * lam_re - nr * lam_im) / den
    bb_re = fr[..., None] * b_re - fi[..., None] * b_im
    bb_im = fr[..., None] * b_im + fi[..., None] * b_re
    c_re, c_im = c_re.astype(F32), c_im.astype(F32)
    lb_re = lp_re[..., None] * bb_re[:, :, None] - lp_im[..., None] * bb_im[:, :, None]
    lb_im = lp_re[..., None] * bb_im[:, :, None] + lp_im[..., None] * bb_re[:, :, None]
    kern = (jnp.einsum('dgon,dgtni->dgtoi', c_re, lb_re[:, :, :C], precision=hi)
            - jnp.einsum('dgon,dgtni->dgtoi', c_im, lb_im[:, :, :C], precision=hi))
    s_idx = jnp.arange(C)[:, None]
    t_idx = jnp.arange(C)[None, :]
    kf = kern[0][:, jnp.clip(t_idx - s_idx, 0, C - 1)]
    kr = kern[1][:, jnp.clip(s_idx - t_idx, 0, C - 1)]
    kf = jnp.where((t_idx >= s_idx)[None, :, :, None, None], kf, 0.0)
    kr = jnp.where((s_idx >= t_idx)[None, :, :, None, None], kr, 0.0)
    skip = (jnp.eye(C, dtype=F32)[None, :, :, None, None]
            * (jnp.eye(P, dtype=F32)[None, None, None] * d_skip.astype(F32).reshape(G, 1, 1, P, 1)))
    m = (kf + kr + skip).transpose(0, 1, 4, 2, 3).reshape(G, C * P, C * P)
    wsf_re = lb_re[0][:, C - 1::-1][:, :C].transpose(0, 1, 3, 2).reshape(G, C * P, N)
    wsf_im = lb_im[0][:, C - 1::-1][:, :C].transpose(0, 1, 3, 2).reshape(G, C * P, N)
    wsr_re = lb_re[1][:, :C].transpose(0, 1, 3, 2).reshape(G, C * P, N)
    wsr_im = lb_im[1][:, :C].transpose(0, 1, 3, 2).reshape(G, C * P, N)
    def state_out(d, powers_re, powers_im):
        cr, ci = c_re[d][:, None], c_im[d][:, None]
        pr, pi_ = powers_re[:, :, None], powers_im[:, :, None]
        from_re = (cr * pr - ci * pi_).transpose(0, 3, 1, 2).reshape(G, N, C * P)
        from_im = (-(cr * pi_ + ci * pr)).transpose(0, 3, 1, 2).reshape(G, N, C * P)
        return from_re, from_im
    of_re, of_im = state_out(0, lp_re[0][:, 1:C + 1], lp_im[0][:, 1:C + 1])
    or_re, or_im = state_out(1, lp_re[1][:, C:0:-1], lp_im[1][:, C:0:-1])
    w_state = jnp.concatenate([_pair_blockdiag(w) for w in (wsf_re, wsf_im, wsr_re, wsr_im)], axis=-1)
    w_out = jnp.concatenate([_pair_blockdiag(w) for w in (m, of_re, of_im, or_re, or_im)], axis=1)
    lam_c = jnp.stack([lp_re[0][:, C], lp_im[0][:, C], lp_re[1][:, C], lp_im[1][:, C]])
    return w_state.astype(BF16), w_out.astype(BF16), lam_c.reshape(4, 1, S5_LANES)


def _s5_state_kernel(u_ref, w_ref, fre_ref, fim_ref, rre_ref, rim_ref):
    r = jnp.dot(u_ref[...], w_ref[...], preferred_element_type=F32)
    for k, ref in enumerate((fre_ref, fim_ref, rre_ref, rim_ref)):
        ref[...] = r[:, k * LANES:(k + 1) * LANES]


def _s5_local_states(u2, w_state):
    uw = 2 * S5_CHUNK * S5_GROUP_CH
    out = jax.ShapeDtypeStruct((S5_ROWS, S5_LANES), F32)
    ospec = pl.BlockSpec((S5_ROWS, LANES), lambda q: (0, q))
    return pl.pallas_call(
        _s5_state_kernel,
        grid=(S5_UNITS,),
        in_specs=[pl.BlockSpec((None, S5_ROWS, uw), lambda q: (q, 0, 0)),
                  pl.BlockSpec((None, uw, 4 * LANES), lambda q: (q, 0, 0))],
        out_specs=[ospec] * 4,
        out_shape=[out] * 4,
        compiler_params=_cparams("parallel"),
    )(u2, w_state)


S5_SCAN_LANES = 512
S5_PAIR_ROWS = 2 * BATCH


def _s5_scan_kernel(fre_ref, fim_ref, rre_ref, rim_ref, lam_ref,
                    pfre_ref, pfim_ref, prre_ref, prim_ref):
    lam = lam_ref[...]
    shape = (S5_PAIR_ROWS, S5_SCAN_LANES)
    top = lax.broadcasted_iota(I32, shape, 0) < BATCH
    n_pairs = S5_NCHUNK // 2
    ctx_pairs = S5_CTX_CHUNKS // 2

    def cmul_add(ar, ai, sr, si, lr, li):
        return ar * sr - ai * si + lr, ar * si + ai * sr + li

    def half_step(ar, ai, sr, si, lr, li, first_top):
        xr, xi = cmul_add(ar, ai, sr, si, lr, li)
        xr, xi = pltpu.roll(xr, BATCH, 0), pltpu.roll(xi, BATCH, 0)
        keep = top if first_top else ~top
        prev_r, prev_i = jnp.where(keep, sr, xr), jnp.where(keep, si, xi)
        yr, yi = cmul_add(ar, ai, xr, xi, lr, li)
        nr = jnp.where(keep, pltpu.roll(yr, BATCH, 0), yr)
        ni = jnp.where(keep, pltpu.roll(yi, BATCH, 0), yi)
        return prev_r, prev_i, nr, ni

    def body(kk, carry):
        sfr, sfi, srr, sri = carry
        fo = pl.multiple_of(kk * S5_PAIR_ROWS, S5_PAIR_ROWS)
        rp = jnp.where(kk < ctx_pairs, ctx_pairs - 1 - kk, n_pairs - 1 + ctx_pairs - kk)
        ro = pl.multiple_of(rp * S5_PAIR_ROWS, S5_PAIR_ROWS)
        pr, pi_, sfr, sfi = half_step(lam[0], lam[1], sfr, sfi,
                                      fre_ref[pl.ds(fo, S5_PAIR_ROWS), :],
                                      fim_ref[pl.ds(fo, S5_PAIR_ROWS), :], True)
        pfre_ref[pl.ds(fo, S5_PAIR_ROWS), :] = pr
        pfim_ref[pl.ds(fo, S5_PAIR_ROWS), :] = pi_
        pr, pi_, srr, sri = half_step(lam[2], lam[3], srr, sri,
                                      rre_ref[pl.ds(ro, S5_PAIR_ROWS), :],
                                      rim_ref[pl.ds(ro, S5_PAIR_ROWS), :], False)
        prre_ref[pl.ds(ro, S5_PAIR_ROWS), :] = pr
        prim_ref[pl.ds(ro, S5_PAIR_ROWS), :] = pi_
        return sfr, sfi, srr, sri

    zero = jnp.zeros(shape, F32)
    lax.fori_loop(0, n_pairs, body, (zero, zero, zero, zero))


def _s5_chunk_scan(loc, lam_c):
    spec = pl.BlockSpec((S5_ROWS, S5_SCAN_LANES), lambda j: (0, j))
    out = jax.ShapeDtypeStruct((S5_ROWS, S5_LANES), F32)
    return pl.pallas_call(
        _s5_scan_kernel,
        grid=(S5_LANES // S5_SCAN_LANES,),
        in_specs=[spec] * 4 + [pl.BlockSpec((4, 1, S5_SCAN_LANES), lambda j: (0, 0, j))],
        out_specs=[spec] * 4,
        out_shape=[out] * 4,
        compiler_params=_cparams("parallel"),
    )(*loc, lam_c)


def _s5_out_kernel(u_ref, fre_ref, fim_ref, rre_ref, rim_ref, w_ref, o_ref):
    lhs = jnp.concatenate([u_ref[...]] + [r[...].astype(BF16)
                                           for r in (fre_ref, fim_ref, rre_ref, rim_ref)], axis=1)
    o_ref[...] = jnp.dot(lhs, w_ref[...], preferred_element_type=F32)


def _s5_outputs(u2, prev, w_out):
    uw = 2 * S5_CHUNK * S5_GROUP_CH
    sspec = pl.BlockSpec((S5_ROWS, LANES), lambda q: (0, q))
    return pl.pallas_call(
        _s5_out_kernel,
        grid=(S5_UNITS,),
        in_specs=[pl.BlockSpec((None, S5_ROWS, uw), lambda q: (q, 0, 0))] + [sspec] * 4
                 + [pl.BlockSpec((None, uw + 4 * LANES, uw), lambda q: (q, 0, 0))],
        out_specs=pl.BlockSpec((None, S5_ROWS, uw), lambda q: (q, 0, 0)),
        out_shape=jax.ShapeDtypeStruct((S5_UNITS, S5_ROWS, uw), F32),
        compiler_params=_cparams("parallel"),
    )(u2, *prev, w_out)


def _glu_kernel(y_ref, w_ref, o_ref):
    y = y_ref[...]
    g = 0.5 * y * (1.0 + jnp.tanh(math.sqrt(2.0 / math.pi) * (y + 0.044715 * (y * y * y))))
    t = jnp.dot(g.astype(BF16), w_ref[...].astype(BF16), preferred_element_type=F32)
    o_ref[...] = (g * _sigmoid(t)).astype(o_ref.dtype)


def _s5_glu(y, w_glu):
    rows = y.shape[0]
    tm = 512
    return pl.pallas_call(
        _glu_kernel,
        grid=(rows // tm,),
        in_specs=[pl.BlockSpec((tm, S5_WIDTH), lambda i: (i, 0)),
                  pl.BlockSpec((S5_WIDTH, S5_WIDTH), lambda i: (0, 0))],
        out_specs=pl.BlockSpec((tm, S5_WIDTH), lambda i: (i, 0)),
        out_shape=jax.ShapeDtypeStruct((rows, S5_WIDTH), BF16),
        compiler_params=_cparams("parallel"),
    )(y, w_glu)


def _s5_branch(z, s5w, w_glu):
    w_state, w_out, lam_c = s5w
    C, P = S5_CHUNK, S5_GROUP_CH
    u = z[:, COL_U:COL_U + S5_WIDTH]
    u_seq = jnp.concatenate([u[N_LAT:].reshape(BATCH, CTX_LEN, S5_WIDTH),
                             u[:N_LAT].reshape(BATCH, SEQ, S5_WIDTH)], axis=1)
    u2 = (u_seq.astype(BF16).reshape(BATCH, S5_NCHUNK, C, S5_UNITS, 2, P)
          .transpose(3, 1, 0, 4, 2, 5).reshape(S5_UNITS, S5_ROWS, 2 * C * P))
    loc = _s5_local_states(u2, w_state)
    prev = _s5_chunk_scan(loc, lam_c)
    y2 = _s5_outputs(u2, prev, w_out)
    y = (y2.reshape(S5_UNITS, S5_NCHUNK, BATCH, 2, C, P)
         .transpose(2, 1, 4, 0, 3, 5).reshape(BATCH, S5_SEQ, S5_WIDTH))
    y_all = jnp.concatenate([y[:, CTX_LEN:].reshape(N_LAT, S5_WIDTH),
                             y[:, :CTX_LEN].reshape(N_CTX, S5_WIDTH)], axis=0)
    return _s5_glu(y_all, w_glu)


def _merge_kernel(ya_ref, ys_ref, yc_ref, wa_ref, ws_ref, wc_ref, ga_ref, gs_ref, gc_ref, o_ref):
    def branch(y_ref, w_ref, g_ref):
        return _sigmoid(g_ref[...]) * jnp.dot(y_ref[...], w_ref[...].astype(BF16),
                                              preferred_element_type=F32)
    o_ref[...] = (branch(ya_ref, wa_ref, ga_ref) + branch(ys_ref, ws_ref, gs_ref)
                  + branch(yc_ref, wc_ref, gc_ref)).astype(o_ref.dtype)


def _gated_merge(ya, ys, yc, z, w_br_a, w_br_s, w_br_c):
    rows = ya.shape[0]
    tm, tn = 512, 512
    aspec = pl.BlockSpec((tm, NA_WIDTH), lambda i, j: (i, 0))
    wspec = pl.BlockSpec((NA_WIDTH, tn), lambda i, j: (0, j))
    gspec = lambda col: pl.BlockSpec((tm, tn), lambda i, j: (i, col // tn + j))
    return pl.pallas_call(
        _merge_kernel,
        grid=(rows // tm, D_MODEL // tn),
        in_specs=[aspec] * 3 + [wspec] * 3 + [gspec(COL_GA), gspec(COL_GS), gspec(COL_GC)],
        out_specs=pl.BlockSpec((tm, tn), lambda i, j: (i, j)),
        out_shape=jax.ShapeDtypeStruct((rows, D_MODEL), BF16),
        compiler_params=_cparams("parallel", "parallel"),
    )(ya, ys, yc, w_br_a, w_br_s, w_br_c, z, z, z)


def _proj_res_kernel(m_ref, w_ref, x_ref, g_ref, o_ref):
    y = jnp.dot(m_ref[...], w_ref[...].astype(BF16), preferred_element_type=F32)
    o_ref[...] = x_ref[...] + g_ref[...] * y


def _proj_residual(m, w_out, x, mod3, gate_idx):
    rows = m.shape[0]
    tm, tn = 1024, 512
    nj = D_MODEL // tn
    return pl.pallas_call(
        _proj_res_kernel,
        grid=(rows // tm, nj),
        in_specs=[
            pl.BlockSpec((tm, D_MODEL), lambda i, j: (i, 0)),
            pl.BlockSpec((D_MODEL, tn), lambda i, j: (0, j)),
            pl.BlockSpec((tm, tn), lambda i, j: (i, j)),
            pl.BlockSpec((None, 1, tn), lambda i, j: (_batch_of_block(i, tm), 0, gate_idx * nj + j)),
        ],
        out_specs=pl.BlockSpec((tm, tn), lambda i, j: (i, j)),
        out_shape=jax.ShapeDtypeStruct((rows, D_MODEL), F32),
        compiler_params=_cparams("parallel", "parallel"),
    )(m, w_out, x, mod3)


def _router_kernel(x_ref, g_ref, sc_ref, sh_ref, wr_ref, br_ref, h_ref, idx_ref, wt_ref):
    x = x_ref[...]
    tm = x.shape[0]
    y = x * lax.rsqrt(jnp.mean(x * x, axis=-1, keepdims=True) + RMS_EPS) * g_ref[...]
    h = y * (1.0 + sc_ref[...]) + sh_ref[...]
    for k in range(ROW_CHUNKS):
        h_ref[pl.ds(k, tm, stride=ROW_CHUNKS), :] = h[:, k * LANES:(k + 1) * LANES]
    logits = jnp.dot(h, wr_ref[...], preferred_element_type=F32,
                     precision=lax.Precision.HIGHEST) + br_ref[...]
    lane = lax.broadcasted_iota(I32, logits.shape, 1)
    vals, idxs = [], []
    for _ in range(TOP_K):
        m = jnp.max(logits, axis=-1, keepdims=True)
        sel = jnp.min(jnp.where(logits == m, lane, LANES), axis=-1, keepdims=True)
        vals.append(m)
        idxs.append(sel)
        logits = jnp.where(lane == sel, NEG, logits)
    es = [jnp.exp(v - vals[0]) for v in vals]
    den = es[0] + es[1] + es[2] + es[3]
    idx_out = jnp.zeros(lane.shape, I32)
    wt_out = jnp.zeros(lane.shape, F32)
    for k in range(TOP_K):
        idx_out = jnp.where(lane == k, idxs[k], idx_out)
        wt_out = jnp.where(lane == k, es[k] / den, wt_out)
    idx_ref[...] = idx_out
    wt_ref[...] = wt_out


def _norm_router(x, g, mod3, shift_idx, scale_idx, w_router, b_router):
    rows = x.shape[0]
    tm = 256
    wr = jnp.zeros((D_MODEL, LANES), F32).at[:, :N_EXPERTS].set(w_router)
    br = jnp.full((1, LANES), NEG, F32).at[0, :N_EXPERTS].set(b_router)
    small = pl.BlockSpec((tm, LANES), lambda i: (i, 0))
    return pl.pallas_call(
        _router_kernel,
        grid=(rows // tm,),
        in_specs=[
            pl.BlockSpec((tm, D_MODEL), lambda i: (i, 0)),
            pl.BlockSpec((1, D_MODEL), lambda i: (0, 0)),
            pl.BlockSpec((None, 1, D_MODEL), lambda i: (_batch_of_block(i, tm), 0, scale_idx)),
            pl.BlockSpec((None, 1, D_MODEL), lambda i: (_batch_of_block(i, tm), 0, shift_idx)),
            pl.BlockSpec((D_MODEL, LANES), lambda i: (0, 0)),
            pl.BlockSpec((1, LANES), lambda i: (0, 0)),
        ],
        out_specs=[pl.BlockSpec((tm * ROW_CHUNKS, LANES), lambda i: (i, 0)), small, small],
        out_shape=[jax.ShapeDtypeStruct((rows * ROW_CHUNKS, LANES), F32),
                   jax.ShapeDtypeStruct((rows, LANES), I32),
                   jax.ShapeDtypeStruct((rows, LANES), F32)],
        compiler_params=_cparams("parallel"),
    )(x, g.reshape(1, D_MODEL), mod3, mod3, wr, br)


def _expert_kernel(be_ref, tok_ref, nused_ref, h_hbm, w1_ref, b1_ref, w2_ref, b2_ref, sw_ref,
                   o_ref, gbuf, xb, sem):
    i = pl.program_id(0)
    n_used = nused_ref[0]
    slot = i % 2

    def row_copy(blk, j, s):
        tok = tok_ref[blk * MOE_BLOCK + j]
        return pltpu.make_async_copy(h_hbm.at[tok], gbuf.at[s, pl.ds(j * ROW_CHUNKS, ROW_CHUNKS), :],
                                     sem.at[s])

    def start_gather(blk, s):
        def issue(j, c):
            row_copy(blk, j, s).start()
            return c
        lax.fori_loop(0, MOE_BLOCK, issue, 0)

    def wait_gather(blk, s):
        def wait(j, c):
            row_copy(blk, j, s).wait()
            return c
        lax.fori_loop(0, MOE_BLOCK, wait, 0)

    @pl.when(i == 0)
    def _():
        start_gather(0, 0)

    @pl.when(i + 1 < n_used)
    def _():
        start_gather(i + 1, 1 - slot)

    @pl.when(i < n_used)
    def _():
        wait_gather(i, slot)
        g = gbuf.at[slot]
        for k in range(ROW_CHUNKS):
            xb[:, k * LANES:(k + 1) * LANES] = g[pl.ds(k, MOE_BLOCK, stride=ROW_CHUNKS), :].astype(BF16)
        a = jnp.dot(xb[...], w1_ref[...], preferred_element_type=F32) + b1_ref[...]
        gate = jnp.minimum(a[:, :D_FF_EXPERT], SWIGLU_LIMIT)
        lin = jnp.clip(a[:, D_FF_EXPERT:], -SWIGLU_LIMIT, SWIGLU_LIMIT)
        act = gate * _sigmoid(SWIGLU_ALPHA * gate) * (lin + 1.0)
        y = jnp.dot(act.astype(BF16), w2_ref[...], preferred_element_type=F32) + b2_ref[...]
        y = y * sw_ref[...]
        for k in range(ROW_CHUNKS):
            o_ref[pl.ds(k, MOE_BLOCK, stride=ROW_CHUNKS), :] = y[:, k * LANES:(k + 1) * LANES]

    @pl.when(i >= n_used)
    def _():
        o_ref[...] = jnp.zeros(o_ref.shape, o_ref.dtype)


def _expert_ffn(h_rows, block_expert, slot_token, n_used, slot_w, w1, b1, w2, b2):
    nb = block_expert.shape[0]
    grid_spec = pltpu.PrefetchScalarGridSpec(
        num_scalar_prefetch=3,
        grid=(nb,),
        in_specs=[
            pl.BlockSpec(memory_space=pl.ANY),
            pl.BlockSpec((None, D_MODEL, 2 * D_FF_EXPERT), lambda i, be, tok, nu: (be[i], 0, 0)),
            pl.BlockSpec((None, 1, 2 * D_FF_EXPERT), lambda i, be, tok, nu: (be[i], 0, 0)),
            pl.BlockSpec((None, D_FF_EXPERT, D_MODEL), lambda i, be, tok, nu: (be[i], 0, 0)),
            pl.BlockSpec((None, 1, D_MODEL), lambda i, be, tok, nu: (be[i], 0, 0)),
            pl.BlockSpec((MOE_BLOCK, 1), lambda i, be, tok, nu: (i, 0)),
        ],
        out_specs=pl.BlockSpec((MOE_BLOCK * ROW_CHUNKS, LANES), lambda i, be, tok, nu: (i, 0)),
        scratch_shapes=[
            pltpu.VMEM((2, MOE_BLOCK * ROW_CHUNKS, LANES), F32),
            pltpu.VMEM((MOE_BLOCK, D_MODEL), BF16),
            pltpu.SemaphoreType.DMA((2,)),
        ],
    )
    return pl.pallas_call(
        _expert_kernel,
        grid_spec=grid_spec,
        out_shape=jax.ShapeDtypeStruct((nb * MOE_BLOCK * ROW_CHUNKS, LANES), F32),
        compiler_params=_cparams("arbitrary"),
    )(block_expert, slot_token, n_used, h_rows, w1, b1, w2, b2, slot_w)


COMB_TM = 128


def _combine_kernel(dest_ref, y_hbm, x_ref, g_ref, o_ref, cbuf, sem):
    i = pl.program_id(0)
    n = pl.num_programs(0)
    slot = i % 2
    n_rows = COMB_TM * TOP_K

    def row_copy(blk, r, s):
        src = dest_ref[blk * n_rows + r]
        pos = (r % TOP_K) * COMB_TM + r // TOP_K
        return pltpu.make_async_copy(y_hbm.at[src], cbuf.at[s, pl.ds(pos * ROW_CHUNKS, ROW_CHUNKS), :],
                                     sem.at[s])

    def start_gather(blk, s):
        def issue(r, c):
            row_copy(blk, r, s).start()
            return c
        lax.fori_loop(0, n_rows, issue, 0)

    def wait_gather(blk, s):
        def wait(r, c):
            row_copy(blk, r, s).wait()
            return c
        lax.fori_loop(0, n_rows, wait, 0)

    @pl.when(i == 0)
    def _():
        start_gather(0, 0)

    @pl.when(i + 1 < n)
    def _():
        start_gather(i + 1, 1 - slot)

    wait_gather(i, slot)
    c = cbuf.at[slot]
    for k in range(ROW_CHUNKS):
        f = c[pl.ds(k, COMB_TM, stride=ROW_CHUNKS), :]
        for kk in range(1, TOP_K):
            f = f + c[pl.ds(kk * COMB_TM * ROW_CHUNKS + k, COMB_TM, stride=ROW_CHUNKS), :]
        sl = slice(k * LANES, (k + 1) * LANES)
        o_ref[:, sl] = x_ref[:, sl] + g_ref[:, sl] * f


def _moe_combine(dest, y_rows, x, mod3, gate_idx):
    rows = x.shape[0]
    grid_spec = pltpu.PrefetchScalarGridSpec(
        num_scalar_prefetch=1,
        grid=(rows // COMB_TM,),
        in_specs=[
            pl.BlockSpec(memory_space=pl.ANY),
            pl.BlockSpec((COMB_TM, D_MODEL), lambda i, d: (i, 0)),
            pl.BlockSpec((None, 1, D_MODEL), lambda i, d: (_batch_of_block(i, COMB_TM), 0, gate_idx)),
        ],
        out_specs=pl.BlockSpec((COMB_TM, D_MODEL), lambda i, d: (i, 0)),
        scratch_shapes=[
            pltpu.VMEM((2, COMB_TM * TOP_K * ROW_CHUNKS, LANES), F32),
            pltpu.SemaphoreType.DMA((2,)),
        ],
    )
    return pl.pallas_call(
        _combine_kernel,
        grid_spec=grid_spec,
        out_shape=jax.ShapeDtypeStruct((rows, D_MODEL), F32),
        compiler_params=_cparams("arbitrary"),
    )(dest, y_rows, x, mod3)


def _moe_ffn(x, g, mod3, w_router, b_router, w1, b1, w2, b2):
    rows = x.shape[0]
    n_assign = rows * TOP_K
    nb = n_assign // MOE_BLOCK + N_EXPERTS
    h_rows, idx_pad, wt_pad = _norm_router(x, g, mod3, 3, 4, w_router, b_router)
    flat_e = idx_pad[:, :TOP_K].reshape(-1)
    flat_w = wt_pad[:, :TOP_K].reshape(-1)
    onehot = (flat_e[:, None] == jnp.arange(N_EXPERTS, dtype=I32)[None, :]).astype(I32)
    csum = jnp.cumsum(onehot, axis=0)
    rank = jnp.sum(csum * onehot, axis=1) - 1
    counts = csum[-1]
    padded = (counts + MOE_BLOCK - 1) // MOE_BLOCK * MOE_BLOCK
    padded_end = jnp.cumsum(padded)
    padded_start = padded_end - padded
    dest = (padded_start[flat_e] + rank).astype(I32)
    block_start = jnp.arange(nb, dtype=I32) * MOE_BLOCK
    block_expert = jnp.minimum(jnp.searchsorted(padded_end, block_start, side='right'),
                               N_EXPERTS - 1).astype(I32)
    n_used = (padded_end[-1:] // MOE_BLOCK).astype(I32)
    slot_token = jnp.zeros((nb * MOE_BLOCK,), I32).at[dest].set(
        jnp.arange(n_assign, dtype=I32) // TOP_K)
    slot_w = jnp.zeros((nb * MOE_BLOCK,), F32).at[dest].set(flat_w).reshape(nb * MOE_BLOCK, 1)
    w1p = jnp.concatenate([w1[..., 0::2], w1[..., 1::2]], axis=-1).astype(BF16)
    b1p = jnp.concatenate([b1[..., 0::2], b1[..., 1::2]], axis=-1).reshape(N_EXPERTS, 1, 2 * D_FF_EXPERT)
    y_rows = _expert_ffn(h_rows.reshape(rows, ROW_CHUNKS, LANES), block_expert, slot_token, n_used,
                         slot_w, w1p, b1p, w2.astype(BF16), b2.reshape(N_EXPERTS, 1, D_MODEL))
    return _moe_combine(dest, y_rows.reshape(nb * MOE_BLOCK, ROW_CHUNKS, LANES), x, mod3, 5)


def kernel(x, c, ctx, c_ctx, norm1_g, norm2_g, w_ada, b_ada, w_in, na_rel_bias, q_norm_g, k_norm_g,
           s5_lam_re, s5_lam_im, s5_log_dt, s5_b_re, s5_b_im, s5_c_re, s5_c_im, s5_d, s5_w_glu,
           w_br_a, w_br_s, w_br_c, w_out, w_router, b_router, w_exp1, b_exp1, w_exp2, b_exp2,
           final_norm_g):
    cc = jnp.concatenate([c, c_ctx[None], jnp.zeros((8 - BATCH - 1, D_MODEL), F32)], axis=0)
    mod = _ada_mod(cc, w_ada, b_ada)
    cos, sin = _rope_tables()
    xa = jnp.concatenate([x.reshape(N_LAT, D_MODEL), ctx.reshape(N_CTX, D_MODEL)], axis=0)
    for l in range(DEPTH):
        ctx_out = l < DEPTH - 1
        mod3 = mod[l].reshape(8, 1, 6 * D_MODEL)
        h = _norm_mod(xa, norm1_g[l], mod3, 0, 1)
        z = _matmul(h, w_in[l], F32)
        pt = _na_bias_table(na_rel_bias[l])
        ya = _na_attention(z, pt)
        qk = _qk_prep(z, q_norm_g[l], k_norm_g[l], cos, sin)
        yc = _gqa_attention(qk, z)
        s5w = _s5_weights(s5_lam_re[l], s5_lam_im[l], s5_log_dt[l], s5_b_re[l], s5_b_im[l],
                          s5_c_re[l], s5_c_im[l], s5_d[l])
        ys = _s5_branch(z, s5w, s5_w_glu[l])
        if ctx_out:
            ya_c = _ctx_attention(z, COL_QA // HEAD_DIM, z, COL_KA // HEAD_DIM, z, COL_VA // HEAD_DIM,
                                  NA_HEADS, 1)
            yc_c = _ctx_attention(qk, 0, qk, GQA_Q_WIDTH // HEAD_DIM, z, COL_VG // HEAD_DIM,
                                  GQA_KV_HEADS, GQA_GROUP)
            ya = jnp.concatenate([ya, ya_c], axis=0)
            yc = jnp.concatenate([yc, yc_c], axis=0)
        m = _gated_merge(ya, ys, yc, z, w_br_a[l], w_br_s[l], w_br_c[l])
        xa = _proj_residual(m, w_out[l], xa, mod3, 2)
        xa = _moe_ffn(xa, norm2_g[l], mod3, w_router[l], b_router[l],
                      w_exp1[l], b_exp1[l], w_exp2[l], b_exp2[l])
    return _final_norm(xa[:N_LAT], final_norm_g).reshape(BATCH, SEQ, D_MODEL)
```

```python
import functools
import math

import jax
import jax.numpy as jnp
from jax import lax
from jax.experimental import pallas as pl
from jax.experimental.pallas import tpu as pltpu

F32 = jnp.float32
BF16 = jnp.bfloat16
I32 = jnp.int32

D_MODEL = 2048
BATCH = 4
SEQ = 2048
DEPTH = 2
GRID_W = 64
GRID_H = SEQ // GRID_W
CTX_LEN = 256
HEAD_DIM = 128
NA_HEADS = 8
NA_WIDTH = NA_HEADS * HEAD_DIM
NA_KH = 8
NA_KW = 16
GQA_HEADS = 8
GQA_KV_HEADS = 2
GQA_GROUP = GQA_HEADS // GQA_KV_HEADS
GQA_Q_WIDTH = GQA_HEADS * HEAD_DIM
GQA_KV_WIDTH = GQA_KV_HEADS * HEAD_DIM
ROPE_THETA = 10000.0
S5_WIDTH = 1024
S5_GROUP_CH = 16
S5_GROUPS = S5_WIDTH // S5_GROUP_CH
S5_STATE = 64
N_EXPERTS = 32
TOP_K = 4
D_FF_EXPERT = 1024
SWIGLU_ALPHA = 1.702
SWIGLU_LIMIT = 7.0
RMS_EPS = 1e-6

N_LAT = BATCH * SEQ
N_CTX = BATCH * CTX_LEN
N_ALL = N_LAT + N_CTX
CTX_WIDTH = 2 * NA_WIDTH + 2 * GQA_KV_WIDTH + S5_WIDTH
IN_WIDTH = CTX_WIDTH + NA_WIDTH + GQA_Q_WIDTH + 3 * D_MODEL
COL_KA = 0
COL_VA = NA_WIDTH
COL_KG = 2 * NA_WIDTH
COL_VG = COL_KG + GQA_KV_WIDTH
COL_U = COL_VG + GQA_KV_WIDTH
COL_QA = CTX_WIDTH
COL_QG = COL_QA + NA_WIDTH
COL_GA = COL_QG + GQA_Q_WIDTH
COL_GS = COL_GA + D_MODEL
COL_GC = COL_GS + D_MODEL

LANES = 128
ROW_CHUNKS = D_MODEL // LANES
NEG = -1e30
ATT_SCALE = HEAD_DIM ** -0.5

S5_CHUNK = 16
S5_SEQ = CTX_LEN + SEQ
S5_NCHUNK = S5_SEQ // S5_CHUNK
S5_CTX_CHUNKS = CTX_LEN // S5_CHUNK
S5_ROWS = S5_NCHUNK * BATCH
S5_UNITS = S5_GROUPS // 2
S5_LANES = S5_GROUPS * S5_STATE

MOE_BLOCK = 256
VMEM_LIMIT = 56 * 1024 * 1024


def _cparams(*sem):
    return pltpu.CompilerParams(dimension_semantics=sem, vmem_limit_bytes=VMEM_LIMIT)


def _sigmoid(x):
    return 1.0 / (1.0 + jnp.exp(-x))


def _batch_of_block(i, rows_per_block):
    return jnp.minimum(i // (SEQ // rows_per_block), BATCH)


def _ada_kernel(c_ref, w_ref, b_ref, o_ref):
    c = c_ref[...]
    a = (c * _sigmoid(c)).astype(BF16)
    o_ref[...] = jnp.dot(a, w_ref[...].astype(BF16), preferred_element_type=F32) + b_ref[...]


def _ada_mod(cc, w_ada, b_ada):
    tn = 1024
    return pl.pallas_call(
        _ada_kernel,
        grid=(DEPTH, 6 * D_MODEL // tn),
        in_specs=[
            pl.BlockSpec((8, D_MODEL), lambda l, j: (0, 0)),
            pl.BlockSpec((None, D_MODEL, tn), lambda l, j: (l, 0, j)),
            pl.BlockSpec((None, 1, tn), lambda l, j: (l, 0, j)),
        ],
        out_specs=pl.BlockSpec((None, 8, tn), lambda l, j: (l, 0, j)),
        out_shape=jax.ShapeDtypeStruct((DEPTH, 8, 6 * D_MODEL), F32),
        compiler_params=_cparams("parallel", "parallel"),
    )(cc, w_ada, b_ada.reshape(DEPTH, 1, 6 * D_MODEL))


def _normmod_kernel(x_ref, g_ref, sc_ref, sh_ref, o_ref):
    x = x_ref[...]
    y = x * lax.rsqrt(jnp.mean(x * x, axis=-1, keepdims=True) + RMS_EPS) * g_ref[...]
    o_ref[...] = (y * (1.0 + sc_ref[...]) + sh_ref[...]).astype(o_ref.dtype)


def _norm_mod(x, g, mod3, shift_idx, scale_idx):
    rows = x.shape[0]
    tm = 256
    return pl.pallas_call(
        _normmod_kernel,
        grid=(rows // tm,),
        in_specs=[
            pl.BlockSpec((tm, D_MODEL), lambda i: (i, 0)),
            pl.BlockSpec((1, D_MODEL), lambda i: (0, 0)),
            pl.BlockSpec((None, 1, D_MODEL), lambda i: (_batch_of_block(i, tm), 0, scale_idx)),
            pl.BlockSpec((None, 1, D_MODEL), lambda i: (_batch_of_block(i, tm), 0, shift_idx)),
        ],
        out_specs=pl.BlockSpec((tm, D_MODEL), lambda i: (i, 0)),
        out_shape=jax.ShapeDtypeStruct((rows, D_MODEL), BF16),
        compiler_params=_cparams("parallel"),
    )(x, g.reshape(1, D_MODEL), mod3, mod3)


def _final_norm_kernel(x_ref, g_ref, o_ref):
    x = x_ref[...]
    o_ref[...] = x * lax.rsqrt(jnp.mean(x * x, axis=-1, keepdims=True) + RMS_EPS) * g_ref[...]


def _final_norm(x, g):
    rows = x.shape[0]
    tm = 256
    return pl.pallas_call(
        _final_norm_kernel,
        grid=(rows // tm,),
        in_specs=[pl.BlockSpec((tm, D_MODEL), lambda i: (i, 0)),
                  pl.BlockSpec((1, D_MODEL), lambda i: (0, 0))],
        out_specs=pl.BlockSpec((tm, D_MODEL), lambda i: (i, 0)),
        out_shape=jax.ShapeDtypeStruct((rows, D_MODEL), F32),
        compiler_params=_cparams("parallel"),
    )(x, g.reshape(1, D_MODEL))


def _mm_kernel(a_ref, w_ref, o_ref):
    o_ref[...] = jnp.dot(a_ref[...], w_ref[...].astype(BF16),
                         preferred_element_type=F32).astype(o_ref.dtype)


def _matmul(a, w, out_dtype, tm=1024, tn=512):
    m, k = a.shape
    n = w.shape[1]
    return pl.pallas_call(
        _mm_kernel,
        grid=(m // tm, n // tn),
        in_specs=[pl.BlockSpec((tm, k), lambda i, j: (i, 0)),
                  pl.BlockSpec((k, tn), lambda i, j: (0, j))],
        out_specs=pl.BlockSpec((tm, tn), lambda i, j: (i, j)),
        out_shape=jax.ShapeDtypeStruct((m, n), out_dtype),
        compiler_params=_cparams("parallel", "parallel"),
    )(a, w)


NA_QROWS = 4
NA_BAND = NA_KH + NA_QROWS
NA_TQ = NA_QROWS * GRID_W
NA_TK = NA_BAND * GRID_W


def _na_bias_table(rel_bias):
    cidx = jnp.arange(GRID_W, dtype=I32)
    col_start = jnp.clip(cidx - NA_KW // 2, 0, GRID_W - NA_KW)
    col_ok = (cidx[None, :] >= col_start[:, None]) & (cidx[None, :] < col_start[:, None] + NA_KW)
    dc = jnp.clip(cidx[None, :] - cidx[:, None] + (NA_KW - 1), 0, 2 * NA_KW - 2)
    t = rel_bias.astype(F32)[:, :, dc]
    t = jnp.where(col_ok[None, None], t, NEG)
    pad = jnp.full((NA_HEADS, 1, GRID_W, GRID_W), NEG, F32)
    t = jnp.concatenate([pad, t, pad], axis=1)
    return jnp.concatenate([t[:, :-1], t[:, 1:]], axis=-1)


def _na_kernel(q_ref, k_ref, v_ref, kc_ref, vc_ref, pt_ref, o_ref):
    i = pl.program_id(2)
    start = jnp.clip(NA_QROWS * i - NA_KH // 2, 0, GRID_H - NA_BAND)
    koff = pl.multiple_of(start * GRID_W, GRID_W * NA_QROWS)
    q = q_ref[...].astype(BF16)
    kb = k_ref[pl.ds(koff, NA_TK), :].astype(BF16)
    vb = v_ref[pl.ds(koff, NA_TK), :].astype(BF16)
    nt = (((1,), (1,)), ((), ()))
    s = lax.dot_general(q, kb, nt, preferred_element_type=F32) * ATT_SCALE
    lane = lax.broadcasted_iota(I32, (GRID_W, 2 * GRID_W), 1)
    bias_rows = []
    for a in range(NA_QROWS):
        rq = NA_QROWS * i + a
        ws = jnp.clip(rq - NA_KH // 2, 0, GRID_H - NA_KH)
        tiles = []
        for jp in range(NA_BAND // 2):
            rk = start + 2 * jp
            e = jnp.clip(rk - rq + NA_KH, 0, 2 * NA_KH - 1)
            ok0 = ((rk >= ws) & (rk < ws + NA_KH)).astype(I32)
            ok1 = ((rk + 1 >= ws) & (rk + 1 < ws + NA_KH)).astype(I32)
            ok = jnp.where(lane < GRID_W, ok0, ok1) > 0
            tiles.append(jnp.where(ok, pt_ref[e], NEG))
        bias_rows.append(jnp.concatenate(tiles, axis=1))
    s = s + jnp.concatenate(bias_rows, axis=0)
    sc = lax.dot_general(q, kc_ref[...].astype(BF16), nt, preferred_element_type=F32) * ATT_SCALE
    m = jnp.maximum(jnp.max(s, axis=-1, keepdims=True), jnp.max(sc, axis=-1, keepdims=True))
    p = jnp.exp(s - m)
    pc = jnp.exp(sc - m)
    den = jnp.sum(p, axis=-1, keepdims=True) + jnp.sum(pc, axis=-1, keepdims=True)
    o = (jnp.dot(p.astype(BF16), vb, preferred_element_type=F32)
         + jnp.dot(pc.astype(BF16), vc_ref[...].astype(BF16), preferred_element_type=F32))
    o_ref[...] = (o / den).astype(o_ref.dtype)


def _na_attention(z, pt):
    qb = SEQ // NA_TQ
    cq, ck, cv = COL_QA // HEAD_DIM, COL_KA // HEAD_DIM, COL_VA // HEAD_DIM
    ctx_blk = N_LAT // CTX_LEN
    return pl.pallas_call(
        _na_kernel,
        grid=(BATCH, NA_HEADS, qb),
        in_specs=[
            pl.BlockSpec((NA_TQ, HEAD_DIM), lambda b, h, i: (b * qb + i, cq + h)),
            pl.BlockSpec((SEQ, HEAD_DIM), lambda b, h, i: (b, ck + h)),
            pl.BlockSpec((SEQ, HEAD_DIM), lambda b, h, i: (b, cv + h)),
            pl.BlockSpec((CTX_LEN, HEAD_DIM), lambda b, h, i: (ctx_blk + b, ck + h)),
            pl.BlockSpec((CTX_LEN, HEAD_DIM), lambda b, h, i: (ctx_blk + b, cv + h)),
            pl.BlockSpec((None, 2 * NA_KH, GRID_W, 2 * GRID_W), lambda b, h, i: (h, 0, 0, 0)),
        ],
        out_specs=pl.BlockSpec((NA_TQ, HEAD_DIM), lambda b, h, i: (b * qb + i, h)),
        out_shape=jax.ShapeDtypeStruct((N_LAT, NA_WIDTH), BF16),
        compiler_params=_cparams("parallel", "parallel", "parallel"),
    )(z, z, z, z, z, pt)


def _softmax_attend(q, ks, vs):
    nt = (((1,), (1,)), ((), ()))
    ss = [lax.dot_general(q, k, nt, preferred_element_type=F32) * ATT_SCALE for k in ks]
    m = ss[0].max(axis=-1, keepdims=True)
    for s in ss[1:]:
        m = jnp.maximum(m, s.max(axis=-1, keepdims=True))
    ps = [jnp.exp(s - m) for s in ss]
    den = ps[0].sum(axis=-1, keepdims=True)
    for p in ps[1:]:
        den = den + p.sum(axis=-1, keepdims=True)
    o = jnp.dot(ps[0].astype(BF16), vs[0], preferred_element_type=F32)
    for p, v in zip(ps[1:], vs[1:]):
        o = o + jnp.dot(p.astype(BF16), v, preferred_element_type=F32)
    return o / den


def _stack_heads(q, group):
    return jnp.concatenate([q[:, g * HEAD_DIM:(g + 1) * HEAD_DIM] for g in range(group)], axis=0)


def _unstack_heads(o, group, rows):
    return jnp.concatenate([o[g * rows:(g + 1) * rows] for g in range(group)], axis=1)


def _ctx_attn_kernel(q_ref, k_ref, v_ref, o_ref, *, group):
    q = _stack_heads(q_ref[...].astype(BF16), group)
    o = _softmax_attend(q, [k_ref[...].astype(BF16)], [v_ref[...].astype(BF16)])
    o_ref[...] = _unstack_heads(o, group, CTX_LEN).astype(o_ref.dtype)


def _ctx_attention(q_arr, q_col, k_arr, k_col, v_arr, v_col, kv_heads, group):
    ctx_blk = N_LAT // CTX_LEN
    qw = group * HEAD_DIM
    return pl.pallas_call(
        functools.partial(_ctx_attn_kernel, group=group),
        grid=(BATCH, kv_heads),
        in_specs=[
            pl.BlockSpec((CTX_LEN, qw), lambda b, h: (ctx_blk + b, q_col // group + h)),
            pl.BlockSpec((CTX_LEN, HEAD_DIM), lambda b, h: (ctx_blk + b, k_col + h)),
            pl.BlockSpec((CTX_LEN, HEAD_DIM), lambda b, h: (ctx_blk + b, v_col + h)),
        ],
        out_specs=pl.BlockSpec((CTX_LEN, qw), lambda b, h: (b, h)),
        out_shape=jax.ShapeDtypeStruct((N_CTX, kv_heads * qw), BF16),
        compiler_params=_cparams("parallel", "parallel"),
    )(q_arr, k_arr, v_arr)


def _rope_tables():
    t = jnp.arange(SEQ, dtype=I32)
    row = (t // GRID_W).astype(F32)
    col = (t % GRID_W).astype(F32)
    n_freq = HEAD_DIM // 4
    inv_freq = ROPE_THETA ** (-jnp.arange(n_freq, dtype=F32) / n_freq)
    ar = row[:, None] * inv_freq
    ac = col[:, None] * inv_freq
    cos = jnp.concatenate([jnp.cos(ar), jnp.cos(ar), jnp.cos(ac), jnp.cos(ac)], axis=-1)
    sin = jnp.concatenate([-jnp.sin(ar), jnp.sin(ar), -jnp.sin(ac), jnp.sin(ac)], axis=-1)
    cos = jnp.concatenate([cos, jnp.ones((CTX_LEN, HEAD_DIM), F32)], axis=0)
    sin = jnp.concatenate([sin, jnp.zeros((CTX_LEN, HEAD_DIM), F32)], axis=0)
    return cos, sin


def _qkprep_kernel(q0_ref, q1_ref, k_ref, cos_ref, sin_ref, qg_ref, kg_ref, o_ref):
    c = cos_ref[...]
    s = sin_ref[...]
    lane = lax.broadcasted_iota(I32, c.shape, 1)
    first = (lane % (HEAD_DIM // 2)) < (HEAD_DIM // 4)

    def prep(x, g):
        y = x * lax.rsqrt(jnp.mean(x * x, axis=-1, keepdims=True) + RMS_EPS) * g
        partner = jnp.where(first, pltpu.roll(y, HEAD_DIM - HEAD_DIM // 4, 1),
                            pltpu.roll(y, HEAD_DIM // 4, 1))
        return (y * c + partner * s).astype(o_ref.dtype)

    half = GQA_HEADS // 2
    for h in range(half):
        sl = slice(h * HEAD_DIM, (h + 1) * HEAD_DIM)
        o_ref[:, sl] = prep(q0_ref[:, sl], qg_ref[...])
        o_ref[:, (half + h) * HEAD_DIM:(half + h + 1) * HEAD_DIM] = prep(q1_ref[:, sl], qg_ref[...])
    for h in range(GQA_KV_HEADS):
        sl = slice(h * HEAD_DIM, (h + 1) * HEAD_DIM)
        o_ref[:, (GQA_HEADS + h) * HEAD_DIM:(GQA_HEADS + h + 1) * HEAD_DIM] = prep(k_ref[:, sl], kg_ref[...])


def _qk_prep(z, q_norm_g, k_norm_g, cos, sin):
    tm = 256
    hq = GQA_Q_WIDTH // 2
    lat_blocks = N_LAT // tm
    per_batch = SEQ // tm
    tbl = lambda i: (jnp.where(i < lat_blocks, i % per_batch, per_batch), 0)
    return pl.pallas_call(
        _qkprep_kernel,
        grid=(N_ALL // tm,),
        in_specs=[
            pl.BlockSpec((tm, hq), lambda i: (i, COL_QG // hq)),
            pl.BlockSpec((tm, hq), lambda i: (i, COL_QG // hq + 1)),
            pl.BlockSpec((tm, GQA_KV_WIDTH), lambda i: (i, COL_KG // GQA_KV_WIDTH)),
            pl.BlockSpec((tm, HEAD_DIM), tbl),
            pl.BlockSpec((tm, HEAD_DIM), tbl),
            pl.BlockSpec((1, HEAD_DIM), lambda i: (0, 0)),
            pl.BlockSpec((1, HEAD_DIM), lambda i: (0, 0)),
        ],
        out_specs=pl.BlockSpec((tm, GQA_Q_WIDTH + GQA_KV_WIDTH), lambda i: (i, 0)),
        out_shape=jax.ShapeDtypeStruct((N_ALL, GQA_Q_WIDTH + GQA_KV_WIDTH), BF16),
        compiler_params=_cparams("parallel"),
    )(z, z, z, cos, sin, q_norm_g.reshape(1, HEAD_DIM), k_norm_g.reshape(1, HEAD_DIM))


GQA_TQ = 256


def _gqa_kernel(q_ref, k_ref, kc_ref, v_ref, vc_ref, o_ref):
    q = _stack_heads(q_ref[...], GQA_GROUP)
    o = _softmax_attend(q, [k_ref[...], kc_ref[...]],
                        [v_ref[...].astype(BF16), vc_ref[...].astype(BF16)])
    o_ref[...] = _unstack_heads(o, GQA_GROUP, GQA_TQ).astype(o_ref.dtype)


def _gqa_attention(qk, z):
    qb = SEQ // GQA_TQ
    qw = GQA_GROUP * HEAD_DIM
    kcol = GQA_Q_WIDTH // HEAD_DIM
    vcol = COL_VG // HEAD_DIM
    ctx_blk = N_LAT // CTX_LEN
    return pl.pallas_call(
        _gqa_kernel,
        grid=(BATCH, GQA_KV_HEADS, qb),
        in_specs=[
            pl.BlockSpec((GQA_TQ, qw), lambda b, h, i: (b * qb + i, h)),
            pl.BlockSpec((SEQ, HEAD_DIM), lambda b, h, i: (b, kcol + h)),
            pl.BlockSpec((CTX_LEN, HEAD_DIM), lambda b, h, i: (ctx_blk + b, kcol + h)),
            pl.BlockSpec((SEQ, HEAD_DIM), lambda b, h, i: (b, vcol + h)),
            pl.BlockSpec((CTX_LEN, HEAD_DIM), lambda b, h, i: (ctx_blk + b, vcol + h)),
        ],
        out_specs=pl.BlockSpec((GQA_TQ, qw), lambda b, h, i: (b * qb + i, h)),
        out_shape=jax.ShapeDtypeStruct((N_LAT, GQA_Q_WIDTH), BF16),
        compiler_params=_cparams("parallel", "parallel", "parallel"),
    )(qk, qk, qk, z, z)


def _pair_blockdiag(x):
    g, a, b = x.shape
    x2 = x.reshape(g // 2, 2, a, b)
    eye = jnp.eye(2, dtype=x.dtype)
    return jnp.einsum('qiab,ij->qiajb', x2, eye).reshape(g // 2, 2 * a, 2 * b)


def _s5_weights(lam_re, lam_im, log_dt, b_re, b_im, c_re, c_im, d_skip):
    hi = lax.Precision.HIGHEST
    G, N, P, C = S5_GROUPS, S5_STATE, S5_GROUP_CH, S5_CHUNK
    lam_re, lam_im = lam_re.astype(F32), lam_im.astype(F32)
    dt = jnp.exp(log_dt.astype(F32))[..., None]
    tau = jnp.arange(C + 1, dtype=F32)[None, None, :, None]
    mag = jnp.exp(lam_re[:, :, None, :] * dt[:, :, None, :] * tau)
    ang = lam_im[:, :, None, :] * dt[:, :, None, :] * tau
    lp_re, lp_im = mag * jnp.cos(ang), mag * jnp.sin(ang)
    nr, ni = lp_re[:, :, 1] - 1.0, lp_im[:, :, 1]
    den = lam_re * lam_re + lam_im * lam_im
    fr = (nr * lam_re + ni * lam_im) / den
    fi = (ni * lam_re - nr * lam_im) / den
    bb_re = fr[..., None] * b_re - fi[..., None] * b_im
    bb_im = fr[..., None] * b_im + fi[..., None] * b_re
    c_re, c_im = c_re.astype(F32), c_im.astype(F32)
    lb_re = lp_re[..., None] * bb_re[:, :, None] - lp_im[..., None] * bb_im[:, :, None]
    lb_im = lp_re[..., None] * bb_im[:, :, None] + lp_im[..., None] * bb_re[:, :, None]
    kern = (jnp.einsum('dgon,dgtni->dgtoi', c_re, lb_re[:, :, :C], precision=hi)
            - jnp.einsum('dgon,dgtni->dgtoi', c_im, lb_im[:, :, :C], precision=hi))
    s_idx = jnp.arange(C)[:, None]
    t_idx = jnp.arange(C)[None, :]
    kf = kern[0][:, jnp.clip(t_idx - s_idx, 0, C - 1)]
    kr = kern[1][:, jnp.clip(s_idx - t_idx, 0, C - 1)]
    kf = jnp.where((t_idx >= s_idx)[None, :, :, None, None], kf, 0.0)
    kr = jnp.where((s_idx >= t_idx)[None, :, :, None, None], kr, 0.0)
    skip = (jnp.eye(C, dtype=F32)[None, :, :, None, None]
            * (jnp.eye(P, dtype=F32)[None, None, None] * d_skip.astype(F32).reshape(G, 1, 1, P, 1)))
    m = (kf + kr + skip).transpose(0, 1, 4, 2, 3).reshape(G, C * P, C * P)
    wsf_re = lb_re[0][:, C - 1::-1][:, :C].transpose(0, 1, 3, 2).reshape(G, C * P, N)
    wsf_im = lb_im[0][:, C - 1::-1][:, :C].transpose(0, 1, 3, 2).reshape(G, C * P, N)
    wsr_re = lb_re[1][:, :C].transpose(0, 1, 3, 2).reshape(G, C * P, N)
    wsr_im = lb_im[1][:, :C].transpose(0, 1, 3, 2).reshape(G, C * P, N)
    def state_out(d, powers_re, powers_im):
        cr, ci = c_re[d][:, None], c_im[d][:, None]
        pr, pi_ = powers_re[:, :, None], powers_im[:, :, None]
        from_re = (cr * pr - ci * pi_).transpose(0, 3, 1, 2).reshape(G, N, C * P)
        from_im = (-(cr * pi_ + ci * pr)).transpose(0, 3, 1, 2).reshape(G, N, C * P)
        return from_re, from_im
    of_re, of_im = state_out(0, lp_re[0][:, 1:C + 1], lp_im[0][:, 1:C + 1])
    or_re, or_im = state_out(1, lp_re[1][:, C:0:-1], lp_im[1][:, C:0:-1])
    w_state = jnp.concatenate([_pair_blockdiag(w) for w in (wsf_re, wsf_im, wsr_re, wsr_im)], axis=-1)
    w_out = jnp.concatenate([_pair_blockdiag(w) for w in (m, of_re, of_im, or_re, or_im)], axis=1)
    lam_c = jnp.stack([lp_re[0][:, C], lp_im[0][:, C], lp_re[1][:, C], lp_im[1][:, C]])
    return w_state.astype(BF16), w_out.astype(BF16), lam_c.reshape(4, 1, S5_LANES)


def _s5_state_kernel(u_ref, w_ref, fre_ref, fim_ref, rre_ref, rim_ref):
    r = jnp.dot(u_ref[...], w_ref[...], preferred_element_type=F32)
    for k, ref in enumerate((fre_ref, fim_ref, rre_ref, rim_ref)):
        ref[...] = r[:, k * LANES:(k + 1) * LANES]


def _s5_local_states(u2, w_state):
    uw = 2 * S5_CHUNK * S5_GROUP_CH
    out = jax.ShapeDtypeStruct((S5_ROWS, S5_LANES), F32)
    ospec = pl.BlockSpec((S5_ROWS, LANES), lambda q: (0, q))
    return pl.pallas_call(
        _s5_state_kernel,
        grid=(S5_UNITS,),
        in_specs=[pl.BlockSpec((None, S5_ROWS, uw), lambda q: (q, 0, 0)),
                  pl.BlockSpec((None, uw, 4 * LANES), lambda q: (q, 0, 0))],
        out_specs=[ospec] * 4,
        out_shape=[out] * 4,
        compiler_params=_cparams("parallel"),
    )(u2, w_state)


S5_SCAN_LANES = 512
S5_PAIR_ROWS = 2 * BATCH


def _s5_scan_kernel(fre_ref, fim_ref, rre_ref, rim_ref, lam_ref,
                    pfre_ref, pfim_ref, prre_ref, prim_ref):
    lam = lam_ref[...]
    shape = (S5_PAIR_ROWS, S5_SCAN_LANES)
    top = lax.broadcasted_iota(I32, shape, 0) < BATCH
    n_pairs = S5_NCHUNK // 2
    ctx_pairs = S5_CTX_CHUNKS // 2

    def cmul_add(ar, ai, sr, si, lr, li):
        return ar * sr - ai * si + lr, ar * si + ai * sr + li

    def half_step(ar, ai, sr, si, lr, li, first_top):
        xr, xi = cmul_add(ar, ai, sr, si, lr, li)
        xr, xi = pltpu.roll(xr, BATCH, 0), pltpu.roll(xi, BATCH, 0)
        keep = top if first_top else ~top
        prev_r, prev_i = jnp.where(keep, sr, xr), jnp.where(keep, si, xi)
        yr, yi = cmul_add(ar, ai, xr, xi, lr, li)
        nr = jnp.where(keep, pltpu.roll(yr, BATCH, 0), yr)
        ni = jnp.where(keep, pltpu.roll(yi, BATCH, 0), yi)
        return prev_r, prev_i, nr, ni

    def body(kk, carry):
        sfr, sfi, srr, sri = carry
        fo = pl.multiple_of(kk * S5_PAIR_ROWS, S5_PAIR_ROWS)
        rp = jnp.where(kk < ctx_pairs, ctx_pairs - 1 - kk, n_pairs - 1 + ctx_pairs - kk)
        ro = pl.multiple_of(rp * S5_PAIR_ROWS, S5_PAIR_ROWS)
        pr, pi_, sfr, sfi = half_step(lam[0], lam[1], sfr, sfi,
                                      fre_ref[pl.ds(fo, S5_PAIR_ROWS), :],
                                      fim_ref[pl.ds(fo, S5_PAIR_ROWS), :], True)
        pfre_ref[pl.ds(fo, S5_PAIR_ROWS), :] = pr
        pfim_ref[pl.ds(fo, S5_PAIR_ROWS), :] = pi_
        pr, pi_, srr, sri = half_step(lam[2], lam[3], srr, sri,
                                      rre_ref[pl.ds(ro, S5_PAIR_ROWS), :],
                                      rim_ref[pl.ds(ro, S5_PAIR_ROWS), :], False)
        prre_ref[pl.ds(ro, S5_PAIR_ROWS), :] = pr
        prim_ref[pl.ds(ro, S5_PAIR_ROWS), :] = pi_
        return sfr, sfi, srr, sri

    zero = jnp.zeros(shape, F32)
    lax.fori_loop(0, n_pairs, body, (zero, zero, zero, zero))


def _s5_chunk_scan(loc, lam_c):
    spec = pl.BlockSpec((S5_ROWS, S5_SCAN_LANES), lambda j: (0, j))
    out = jax.ShapeDtypeStruct((S5_ROWS, S5_LANES), F32)
    return pl.pallas_call(
        _s5_scan_kernel,
        grid=(S5_LANES // S5_SCAN_LANES,),
        in_specs=[spec] * 4 + [pl.BlockSpec((4, 1, S5_SCAN_LANES), lambda j: (0, 0, j))],
        out_specs=[spec] * 4,
        out_shape=[out] * 4,
        compiler_params=_cparams("parallel"),
    )(*loc, lam_c)


def _s5_out_kernel(u_ref, fre_ref, fim_ref, rre_ref, rim_ref, w_ref, o_ref):
    lhs = jnp.concatenate([u_ref[...]] + [r[...].astype(BF16)
                                           for r in (fre_ref, fim_ref, rre_ref, rim_ref)], axis=1)
    o_ref[...] = jnp.dot(lhs, w_ref[...], preferred_element_type=F32)


def _s5_outputs(u2, prev, w_out):
    uw = 2 * S5_CHUNK * S5_GROUP_CH
    sspec = pl.BlockSpec((S5_ROWS, LANES), lambda q: (0, q))
    return pl.pallas_call(
        _s5_out_kernel,
        grid=(S5_UNITS,),
        in_specs=[pl.BlockSpec((None, S5_ROWS, uw), lambda q: (q, 0, 0))] + [sspec] * 4
                 + [pl.BlockSpec((None, uw + 4 * LANES, uw), lambda q: (q, 0, 0))],
        out_specs=pl.BlockSpec((None, S5_ROWS, uw), lambda q: (q, 0, 0)),
        out_shape=jax.ShapeDtypeStruct((S5_UNITS, S5_ROWS, uw), F32),
        compiler_params=_cparams("parallel"),
    )(u2, *prev, w_out)


def _glu_kernel(y_ref, w_ref, o_ref):
    y = y_ref[...]
    g = 0.5 * y * (1.0 + jnp.tanh(math.sqrt(2.0 / math.pi) * (y + 0.044715 * (y * y * y))))
    t = jnp.dot(g.astype(BF16), w_ref[...].astype(BF16), preferred_element_type=F32)
    o_ref[...] = (g * _sigmoid(t)).astype(o_ref.dtype)


def _s5_glu(y, w_glu):
    rows = y.shape[0]
    tm = 512
    return pl.pallas_call(
        _glu_kernel,
        grid=(rows // tm,),
        in_specs=[pl.BlockSpec((tm, S5_WIDTH), lambda i: (i, 0)),
                  pl.BlockSpec((S5_WIDTH, S5_WIDTH), lambda i: (0, 0))],
        out_specs=pl.BlockSpec((tm, S5_WIDTH), lambda i: (i, 0)),
        out_shape=jax.ShapeDtypeStruct((rows, S5_WIDTH), BF16),
        compiler_params=_cparams("parallel"),
    )(y, w_glu)


def _s5_branch(z, s5w, w_glu):
    w_state, w_out, lam_c = s5w
    C, P = S5_CHUNK, S5_GROUP_CH
    u = z[:, COL_U:COL_U + S5_WIDTH]
    u_seq = jnp.concatenate([u[N_LAT:].reshape(BATCH, CTX_LEN, S5_WIDTH),
                             u[:N_LAT].reshape(BATCH, SEQ, S5_WIDTH)], axis=1)
    u2 = (u_seq.astype(BF16).reshape(BATCH, S5_NCHUNK, C, S5_UNITS, 2, P)
          .transpose(3, 1, 0, 4, 2, 5).reshape(S5_UNITS, S5_ROWS, 2 * C * P))
    loc = _s5_local_states(u2, w_state)
    prev = _s5_chunk_scan(loc, lam_c)
    y2 = _s5_outputs(u2, prev, w_out)
    y = (y2.reshape(S5_UNITS, S5_NCHUNK, BATCH, 2, C, P)
         .transpose(2, 1, 4, 0, 3, 5).reshape(BATCH, S5_SEQ, S5_WIDTH))
    y_all = jnp.concatenate([y[:, CTX_LEN:].reshape(N_LAT, S5_WIDTH),
                             y[:, :CTX_LEN].reshape(N_CTX, S5_WIDTH)], axis=0)
    return _s5_glu(y_all, w_glu)


def _merge_kernel(ya_ref, ys_ref, yc_ref, wa_ref, ws_ref, wc_ref, ga_ref, gs_ref, gc_ref, o_ref):
    def branch(y_ref, w_ref, g_ref):
        return _sigmoid(g_ref[...]) * jnp.dot(y_ref[...], w_ref[...].astype(BF16),
                                              preferred_element_type=F32)
    o_ref[...] = (branch(ya_ref, wa_ref, ga_ref) + branch(ys_ref, ws_ref, gs_ref)
                  + branch(yc_ref, wc_ref, gc_ref)).astype(o_ref.dtype)


def _gated_merge(ya, ys, yc, z, w_br_a, w_br_s, w_br_c):
    rows = ya.shape[0]
    tm, tn = 512, 512
    aspec = pl.BlockSpec((tm, NA_WIDTH), lambda i, j: (i, 0))
    wspec = pl.BlockSpec((NA_WIDTH, tn), lambda i, j: (0, j))
    gspec = lambda col: pl.BlockSpec((tm, tn), lambda i, j: (i, col // tn + j))
    return pl.pallas_call(
        _merge_kernel,
        grid=(rows // tm, D_MODEL // tn),
        in_specs=[aspec] * 3 + [wspec] * 3 + [gspec(COL_GA), gspec(COL_GS), gspec(COL_GC)],
        out_specs=pl.BlockSpec((tm, tn), lambda i, j: (i, j)),
        out_shape=jax.ShapeDtypeStruct((rows, D_MODEL), BF16),
        compiler_params=_cparams("parallel", "parallel"),
    )(ya, ys, yc, w_br_a, w_br_s, w_br_c, z, z, z)


def _proj_res_kernel(m_ref, w_ref, x_ref, g_ref, o_ref):
    y = jnp.dot(m_ref[...], w_ref[...].astype(BF16), preferred_element_type=F32)
    o_ref[...] = x_ref[...] + g_ref[...] * y


def _proj_residual(m, w_out, x, mod3, gate_idx):
    rows = m.shape[0]
    tm, tn = 1024, 512
    nj = D_MODEL // tn
    return pl.pallas_call(
        _proj_res_kernel,
        grid=(rows // tm, nj),
        in_specs=[
            pl.BlockSpec((tm, D_MODEL), lambda i, j: (i, 0)),
            pl.BlockSpec((D_MODEL, tn), lambda i, j: (0, j)),
            pl.BlockSpec((tm, tn), lambda i, j: (i, j)),
            pl.BlockSpec((None, 1, tn), lambda i, j: (_batch_of_block(i, tm), 0, gate_idx * nj + j)),
        ],
        out_specs=pl.BlockSpec((tm, tn), lambda i, j: (i, j)),
        out_shape=jax.ShapeDtypeStruct((rows, D_MODEL), F32),
        compiler_params=_cparams("parallel", "parallel"),
    )(m, w_out, x, mod3)


def _router_kernel(x_ref, g_ref, sc_ref, sh_ref, wr_ref, br_ref, h_ref, idx_ref, wt_ref, rank_ref,
                   cnt_ref, count):
    @pl.when(pl.program_id(0) == 0)
    def _():
        count[...] = jnp.zeros(count.shape, count.dtype)

    x = x_ref[...]
    tm = x.shape[0]
    y = x * lax.rsqrt(jnp.mean(x * x, axis=-1, keepdims=True) + RMS_EPS) * g_ref[...]
    h = y * (1.0 + sc_ref[...]) + sh_ref[...]
    for k in range(ROW_CHUNKS):
        h_ref[pl.ds(k, tm, stride=ROW_CHUNKS), :] = h[:, k * LANES:(k + 1) * LANES]
    logits = jnp.dot(h, wr_ref[...], preferred_element_type=F32,
                     precision=lax.Precision.HIGHEST) + br_ref[...]
    lane = lax.broadcasted_iota(I32, logits.shape, 1)
    vals, idxs = [], []
    for _ in range(TOP_K):
        m = jnp.max(logits, axis=-1, keepdims=True)
        sel = jnp.min(jnp.where(logits == m, lane, LANES), axis=-1, keepdims=True)
        vals.append(m)
        idxs.append(sel)
        logits = jnp.where(lane == sel, NEG, logits)
    es = [jnp.exp(v - vals[0]) for v in vals]
    den = es[0] + es[1] + es[2] + es[3]
    hits = [lane == idxs[k] for k in range(TOP_K)]
    per_expert = hits[0].astype(F32)
    for k in range(1, TOP_K):
        per_expert = per_expert + hits[k].astype(F32)
    tri = (lax.broadcasted_iota(I32, (tm, tm), 1) < lax.broadcasted_iota(I32, (tm, tm), 0)).astype(BF16)
    before = jnp.dot(tri, per_expert.astype(BF16), preferred_element_type=F32) + count[...]
    idx_out = jnp.zeros(lane.shape, I32)
    wt_out = jnp.zeros(lane.shape, F32)
    rank_out = jnp.zeros(lane.shape, I32)
    for k in range(TOP_K):
        rank_k = jnp.sum(jnp.where(hits[k], before, 0.0), axis=-1, keepdims=True).astype(I32)
        idx_out = jnp.where(lane == k, idxs[k], idx_out)
        wt_out = jnp.where(lane == k, es[k] / den, wt_out)
        rank_out = jnp.where(lane == k, rank_k, rank_out)
    idx_ref[...] = idx_out
    wt_ref[...] = wt_out
    rank_ref[...] = rank_out
    count[...] = count[...] + jnp.sum(per_expert, axis=0, keepdims=True)
    cnt_ref[...] = count[...]


def _norm_router(x, g, mod3, shift_idx, scale_idx, w_router, b_router):
    rows = x.shape[0]
    tm = 256
    wr = jnp.zeros((D_MODEL, LANES), F32).at[:, :N_EXPERTS].set(w_router)
    br = jnp.full((1, LANES), NEG, F32).at[0, :N_EXPERTS].set(b_router)
    small = pl.BlockSpec((tm, LANES), lambda i: (i, 0))
    return pl.pallas_call(
        _router_kernel,
        grid=(rows // tm,),
        in_specs=[
            pl.BlockSpec((tm, D_MODEL), lambda i: (i, 0)),
            pl.BlockSpec((1, D_MODEL), lambda i: (0, 0)),
            pl.BlockSpec((None, 1, D_MODEL), lambda i: (_batch_of_block(i, tm), 0, scale_idx)),
            pl.BlockSpec((None, 1, D_MODEL), lambda i: (_batch_of_block(i, tm), 0, shift_idx)),
            pl.BlockSpec((D_MODEL, LANES), lambda i: (0, 0)),
            pl.BlockSpec((1, LANES), lambda i: (0, 0)),
        ],
        out_specs=[pl.BlockSpec((tm * ROW_CHUNKS, LANES), lambda i: (i, 0)), small, small, small,
                   pl.BlockSpec((1, LANES), lambda i: (0, 0))],
        out_shape=[jax.ShapeDtypeStruct((rows * ROW_CHUNKS, LANES), F32),
                   jax.ShapeDtypeStruct((rows, LANES), I32),
                   jax.ShapeDtypeStruct((rows, LANES), F32),
                   jax.ShapeDtypeStruct((rows, LANES), I32),
                   jax.ShapeDtypeStruct((1, LANES), F32)],
        scratch_shapes=[pltpu.VMEM((1, LANES), F32)],
        compiler_params=_cparams("arbitrary"),
    )(x, g.reshape(1, D_MODEL), mod3, mod3, wr, br)


def _pair_merge_rows(w2):
    e, f, d = w2.shape
    half = LANES // 2
    return w2.reshape(e, f // LANES, 2, half, d).transpose(0, 1, 3, 2, 4).reshape(e, f, d)


def _expert_kernel(be_ref, tok_ref, nused_ref, h_hbm, w1_ref, b1_ref, w2_ref, b2_ref,
                   o_ref, gbuf, xb, sem):
    i = pl.program_id(0)
    n_used = nused_ref[0]
    slot = i % 2

    def start_gather(blk, s):
        base = blk * MOE_BLOCK

        def issue(j, c):
            pltpu.make_async_copy(h_hbm.at[tok_ref[base + j]],
                                  gbuf.at[s, pl.ds(j * ROW_CHUNKS, ROW_CHUNKS), :], sem.at[s]).start()
            return c
        lax.fori_loop(0, MOE_BLOCK, issue, 0, unroll=8)

    def wait_gather(s):
        def wait(j, c):
            pltpu.make_async_copy(h_hbm.at[0], gbuf.at[s, pl.ds(j * ROW_CHUNKS, ROW_CHUNKS), :],
                                  sem.at[s]).wait()
            return c
        lax.fori_loop(0, MOE_BLOCK, wait, 0, unroll=8)

    @pl.when(i == 0)
    def _():
        start_gather(0, 0)

    @pl.when(i + 1 < n_used)
    def _():
        start_gather(i + 1, 1 - slot)

    @pl.when(i < n_used)
    def _():
        wait_gather(slot)
        g = gbuf.at[slot]
        for k in range(ROW_CHUNKS):
            xb[:, k * LANES:(k + 1) * LANES] = g[pl.ds(k, MOE_BLOCK, stride=ROW_CHUNKS), :].astype(BF16)
        a = jnp.dot(xb[...], w1_ref[...], preferred_element_type=F32) + b1_ref[...]
        even = lax.broadcasted_iota(I32, (MOE_BLOCK, LANES), 1) % 2 == 0
        prods = []
        for s in range(2 * D_FF_EXPERT // LANES):
            a_s = a[:, s * LANES:(s + 1) * LANES]
            gate = jnp.minimum(a_s, SWIGLU_LIMIT)
            lin = jnp.clip(a_s, -SWIGLU_LIMIT, SWIGLU_LIMIT) + 1.0
            prods.append(gate * _sigmoid(SWIGLU_ALPHA * gate) * pltpu.roll(lin, LANES - 1, 1))
        acts = [jnp.where(even, prods[2 * t], pltpu.roll(prods[2 * t + 1], 1, 1))
                for t in range(D_FF_EXPERT // LANES)]
        act = jnp.concatenate(acts, axis=1).astype(BF16)
        y = jnp.dot(act, w2_ref[...], preferred_element_type=F32) + b2_ref[...]
        for k in range(ROW_CHUNKS):
            o_ref[pl.ds(k, MOE_BLOCK, stride=ROW_CHUNKS), :] = y[:, k * LANES:(k + 1) * LANES]

    @pl.when(i >= n_used)
    def _():
        o_ref[...] = jnp.zeros(o_ref.shape, o_ref.dtype)


def _expert_ffn(h_rows, block_expert, slot_token, n_used, w1, b1, w2, b2):
    nb = block_expert.shape[0]
    rows = h_rows.shape[0] // ROW_CHUNKS
    grid_spec = pltpu.PrefetchScalarGridSpec(
        num_scalar_prefetch=3,
        grid=(nb,),
        in_specs=[
            pl.BlockSpec(memory_space=pl.ANY),
            pl.BlockSpec((None, D_MODEL, 2 * D_FF_EXPERT), lambda i, be, tok, nu: (be[i], 0, 0)),
            pl.BlockSpec((None, 1, 2 * D_FF_EXPERT), lambda i, be, tok, nu: (be[i], 0, 0)),
            pl.BlockSpec((None, D_FF_EXPERT, D_MODEL), lambda i, be, tok, nu: (be[i], 0, 0)),
            pl.BlockSpec((None, 1, D_MODEL), lambda i, be, tok, nu: (be[i], 0, 0)),
        ],
        out_specs=pl.BlockSpec((MOE_BLOCK * ROW_CHUNKS, LANES), lambda i, be, tok, nu: (i, 0)),
        scratch_shapes=[
            pltpu.VMEM((2, MOE_BLOCK * ROW_CHUNKS, LANES), F32),
            pltpu.VMEM((MOE_BLOCK, D_MODEL), BF16),
            pltpu.SemaphoreType.DMA((2,)),
        ],
    )
    return pl.pallas_call(
        _expert_kernel,
        grid_spec=grid_spec,
        out_shape=jax.ShapeDtypeStruct((nb * MOE_BLOCK * ROW_CHUNKS, LANES), F32),
        compiler_params=_cparams("arbitrary"),
    )(block_expert, slot_token, n_used, h_rows.reshape(rows, ROW_CHUNKS, LANES), w1, b1, w2, b2)


COMB_TM = 128


def _combine_kernel(src_ref, y_hbm, x_ref, g_ref, wt_ref, o_ref, cbuf, sem):
    i = pl.program_id(0)
    n = pl.num_programs(0)
    slot = i % 2
    n_rows = COMB_TM * TOP_K

    def start_gather(blk, s):
        base = blk * n_rows

        def issue(r, c):
            pltpu.make_async_copy(y_hbm.at[src_ref[base + r]],
                                  cbuf.at[s, pl.ds(r * ROW_CHUNKS, ROW_CHUNKS), :], sem.at[s]).start()
            return c
        lax.fori_loop(0, n_rows, issue, 0, unroll=8)

    @pl.when(i == 0)
    def _():
        start_gather(0, 0)

    @pl.when(i + 1 < n)
    def _():
        start_gather(i + 1, 1 - slot)

    def wait(r, c):
        pltpu.make_async_copy(y_hbm.at[0], cbuf.at[slot, pl.ds(r * ROW_CHUNKS, ROW_CHUNKS), :],
                              sem.at[slot]).wait()
        return c
    lax.fori_loop(0, n_rows, wait, 0, unroll=8)
    c = cbuf.at[slot]
    wt = wt_ref[...]
    wk = [jnp.broadcast_to(wt[:, kk:kk + 1], (COMB_TM, LANES)) for kk in range(TOP_K)]
    for k in range(ROW_CHUNKS):
        f = wk[0] * c[pl.ds(k, COMB_TM, stride=ROW_CHUNKS), :]
        for kk in range(1, TOP_K):
            f = f + wk[kk] * c[pl.ds(kk * COMB_TM * ROW_CHUNKS + k, COMB_TM, stride=ROW_CHUNKS), :]
        sl = slice(k * LANES, (k + 1) * LANES)
        o_ref[:, sl] = x_ref[:, sl] + g_ref[:, sl] * f


def _moe_combine(src, y_rows, wt_pad, x, mod3, gate_idx):
    rows = x.shape[0]
    n_slots = y_rows.shape[0] // ROW_CHUNKS
    grid_spec = pltpu.PrefetchScalarGridSpec(
        num_scalar_prefetch=1,
        grid=(rows // COMB_TM,),
        in_specs=[
            pl.BlockSpec(memory_space=pl.ANY),
            pl.BlockSpec((COMB_TM, D_MODEL), lambda i, d: (i, 0)),
            pl.BlockSpec((None, 1, D_MODEL), lambda i, d: (_batch_of_block(i, COMB_TM), 0, gate_idx)),
            pl.BlockSpec((COMB_TM, LANES), lambda i, d: (i, 0)),
        ],
        out_specs=pl.BlockSpec((COMB_TM, D_MODEL), lambda i, d: (i, 0)),
        scratch_shapes=[
            pltpu.VMEM((2, COMB_TM * TOP_K * ROW_CHUNKS, LANES), F32),
            pltpu.SemaphoreType.DMA((2,)),
        ],
    )
    return pl.pallas_call(
        _combine_kernel,
        grid_spec=grid_spec,
        out_shape=jax.ShapeDtypeStruct((rows, D_MODEL), F32),
        compiler_params=_cparams("arbitrary"),
    )(src, y_rows.reshape(n_slots, ROW_CHUNKS, LANES), x, mod3, wt_pad)


def _moe_ffn(x, g, mod3, w_router, b_router, w1, b1, w2, b2):
    rows = x.shape[0]
    n_assign = rows * TOP_K
    nb = n_assign // MOE_BLOCK + N_EXPERTS
    h_rows, idx_pad, wt_pad, rank_pad, cnt = _norm_router(x, g, mod3, 3, 4, w_router, b_router)
    experts = jnp.arange(N_EXPERTS, dtype=I32)
    counts = cnt[0, :N_EXPERTS].astype(I32)
    padded = (counts + MOE_BLOCK - 1) // MOE_BLOCK * MOE_BLOCK
    padded_end = jnp.cumsum(padded)
    padded_start = padded_end - padded
    idx = idx_pad[:, :TOP_K]
    start_of = jnp.sum(jnp.where(idx[..., None] == experts, padded_start, 0), axis=-1)
    dest = start_of + rank_pad[:, :TOP_K]
    block_start = jnp.arange(nb, dtype=I32) * MOE_BLOCK
    block_expert = jnp.minimum(jnp.sum((block_start[:, None] >= padded_end[None, :]).astype(I32), axis=1),
                               N_EXPERTS - 1)
    n_used = padded_end[-1:] // MOE_BLOCK
    slot_token = jnp.zeros((nb * MOE_BLOCK,), I32).at[dest.reshape(-1)].set(
        jnp.arange(n_assign, dtype=I32) // TOP_K)
    y_rows = _expert_ffn(h_rows, block_expert, slot_token, n_used,
                         w1.astype(BF16), b1.reshape(N_EXPERTS, 1, 2 * D_FF_EXPERT),
                         _pair_merge_rows(w2).astype(BF16), b2.reshape(N_EXPERTS, 1, D_MODEL))
    src = dest.reshape(rows // COMB_TM, COMB_TM, TOP_K).transpose(0, 2, 1).reshape(-1)
    return _moe_combine(src, y_rows, wt_pad, x, mod3, 5)


def kernel(x, c, ctx, c_ctx, norm1_g, norm2_g, w_ada, b_ada, w_in, na_rel_bias, q_norm_g, k_norm_g,
           s5_lam_re, s5_lam_im, s5_log_dt, s5_b_re, s5_b_im, s5_c_re, s5_c_im, s5_d, s5_w_glu,
           w_br_a, w_br_s, w_br_c, w_out, w_router, b_router, w_exp1, b_exp1, w_exp2, b_exp2,
           final_norm_g):
    cc = jnp.concatenate([c, c_ctx[None], jnp.zeros((8 - BATCH - 1, D_MODEL), F32)], axis=0)
    mod = _ada_mod(cc, w_ada, b_ada)
    cos, sin = _rope_tables()
    xa = jnp.concatenate([x.reshape(N_LAT, D_MODEL), ctx.reshape(N_CTX, D_MODEL)], axis=0)
    for l in range(DEPTH):
        ctx_out = l < DEPTH - 1
        mod3 = mod[l].reshape(8, 1, 6 * D_MODEL)
        h = _norm_mod(xa, norm1_g[l], mod3, 0, 1)
        z = _matmul(h, w_in[l], F32)
        pt = _na_bias_table(na_rel_bias[l])
        ya = _na_attention(z, pt)
        qk = _qk_prep(z, q_norm_g[l], k_norm_g[l], cos, sin)
        yc = _gqa_attention(qk, z)
        s5w = _s5_weights(s5_lam_re[l], s5_lam_im[l], s5_log_dt[l], s5_b_re[l], s5_b_im[l],
                          s5_c_re[l], s5_c_im[l], s5_d[l])
        ys = _s5_branch(z, s5w, s5_w_glu[l])
        if ctx_out:
            ya_c = _ctx_attention(z, COL_QA // HEAD_DIM, z, COL_KA // HEAD_DIM, z, COL_VA // HEAD_DIM,
                                  NA_HEADS, 1)
            yc_c = _ctx_attention(qk, 0, qk, GQA_Q_WIDTH // HEAD_DIM, z, COL_VG // HEAD_DIM,
                                  GQA_KV_HEADS, GQA_GROUP)
            ya = jnp.concatenate([ya, ya_c], axis=0)
            yc = jnp.concatenate([yc, yc_c], axis=0)
        m = _gated_merge(ya, ys, yc, z, w_br_a[l], w_br_s[l], w_br_c[l])
        xa = _proj_residual(m, w_out[l], xa, mod3, 2)
        xa = _moe_ffn(xa, norm2_g[l], mod3, w_router[l], b_router[l],
                      w_exp1[l], b_exp1[l], w_exp2[l], b_exp2[l])
    return _final_norm(xa[:N_LAT], final_norm_g).reshape(BATCH, SEQ, D_MODEL)
```

```python
import functools
import math

import jax
import jax.numpy as jnp
from jax import lax
from jax.experimental import pallas as pl
from jax.experimental.pallas import tpu as pltpu

F32 = jnp.float32
BF16 = jnp.bfloat16
I32 = jnp.int32

D_MODEL = 2048
BATCH = 4
SEQ = 2048
DEPTH = 2
GRID_W = 64
GRID_H = SEQ // GRID_W
CTX_LEN = 256
HEAD_DIM = 128
NA_HEADS = 8
NA_WIDTH = NA_HEADS * HEAD_DIM
NA_KH = 8
NA_KW = 16
GQA_HEADS = 8
GQA_KV_HEADS = 2
GQA_GROUP = GQA_HEADS // GQA_KV_HEADS
GQA_Q_WIDTH = GQA_HEADS * HEAD_DIM
GQA_KV_WIDTH = GQA_KV_HEADS * HEAD_DIM
ROPE_THETA = 10000.0
S5_WIDTH = 1024
S5_GROUP_CH = 16
S5_GROUPS = S5_WIDTH // S5_GROUP_CH
S5_STATE = 64
N_EXPERTS = 32
TOP_K = 4
D_FF_EXPERT = 1024
SWIGLU_ALPHA = 1.702
SWIGLU_LIMIT = 7.0
RMS_EPS = 1e-6

N_LAT = BATCH * SEQ
N_CTX = BATCH * CTX_LEN
N_ALL = N_LAT + N_CTX
CTX_WIDTH = 2 * NA_WIDTH + 2 * GQA_KV_WIDTH + S5_WIDTH
IN_WIDTH = CTX_WIDTH + NA_WIDTH + GQA_Q_WIDTH + 3 * D_MODEL
COL_KA = 0
COL_VA = NA_WIDTH
COL_KG = 2 * NA_WIDTH
COL_VG = COL_KG + GQA_KV_WIDTH
COL_U = COL_VG + GQA_KV_WIDTH
COL_QA = CTX_WIDTH
COL_QG = COL_QA + NA_WIDTH
COL_GA = COL_QG + GQA_Q_WIDTH
COL_GS = COL_GA + D_MODEL
COL_GC = COL_GS + D_MODEL

LANES = 128
ROW_CHUNKS = D_MODEL // LANES
NEG = -1e30
ATT_SCALE = HEAD_DIM ** -0.5

S5_CHUNK = 16
S5_SEQ = CTX_LEN + SEQ
S5_NCHUNK = S5_SEQ // S5_CHUNK
S5_CTX_CHUNKS = CTX_LEN // S5_CHUNK
S5_ROWS = S5_NCHUNK * BATCH
S5_UNITS = S5_GROUPS // 2
S5_LANES = S5_GROUPS * S5_STATE

MOE_BLOCK = 256
VMEM_LIMIT = 56 * 1024 * 1024


def _cparams(*sem):
    return pltpu.CompilerParams(dimension_semantics=sem, vmem_limit_bytes=VMEM_LIMIT)


def _sigmoid(x):
    return 1.0 / (1.0 + jnp.exp(-x))


def _batch_of_block(i, rows_per_block):
    return jnp.minimum(i // (SEQ // rows_per_block), BATCH)


def _ada_kernel(c_ref, w_ref, b_ref, o_ref):
    c = c_ref[...]
    a = (c * _sigmoid(c)).astype(BF16)
    o_ref[...] = jnp.dot(a, w_ref[...].astype(BF16), preferred_element_type=F32) + b_ref[...]


def _ada_mod(cc, w_ada, b_ada):
    tn = 1024
    return pl.pallas_call(
        _ada_kernel,
        grid=(DEPTH, 6 * D_MODEL // tn),
        in_specs=[
            pl.BlockSpec((8, D_MODEL), lambda l, j: (0, 0)),
            pl.BlockSpec((None, D_MODEL, tn), lambda l, j: (l, 0, j)),
            pl.BlockSpec((None, 1, tn), lambda l, j: (l, 0, j)),
        ],
        out_specs=pl.BlockSpec((None, 8, tn), lambda l, j: (l, 0, j)),
        out_shape=jax.ShapeDtypeStruct((DEPTH, 8, 6 * D_MODEL), F32),
        compiler_params=_cparams("parallel", "parallel"),
    )(cc, w_ada, b_ada.reshape(DEPTH, 1, 6 * D_MODEL))


def _normmod_kernel(x_ref, g_ref, sc_ref, sh_ref, o_ref):
    x = x_ref[...]
    y = x * lax.rsqrt(jnp.mean(x * x, axis=-1, keepdims=True) + RMS_EPS) * g_ref[...]
    o_ref[...] = (y * (1.0 + sc_ref[...]) + sh_ref[...]).astype(o_ref.dtype)


def _norm_mod(x, g, mod3, shift_idx, scale_idx):
    rows = x.shape[0]
    tm = 256
    return pl.pallas_call(
        _normmod_kernel,
        grid=(rows // tm,),
        in_specs=[
            pl.BlockSpec((tm, D_MODEL), lambda i: (i, 0)),
            pl.BlockSpec((1, D_MODEL), lambda i: (0, 0)),
            pl.BlockSpec((None, 1, D_MODEL), lambda i: (_batch_of_block(i, tm), 0, scale_idx)),
            pl.BlockSpec((None, 1, D_MODEL), lambda i: (_batch_of_block(i, tm), 0, shift_idx)),
        ],
        out_specs=pl.BlockSpec((tm, D_MODEL), lambda i: (i, 0)),
        out_shape=jax.ShapeDtypeStruct((rows, D_MODEL), BF16),
        compiler_params=_cparams("parallel"),
    )(x, g.reshape(1, D_MODEL), mod3, mod3)


def _final_norm_kernel(x_ref, g_ref, o_ref):
    x = x_ref[...]
    o_ref[...] = x * lax.rsqrt(jnp.mean(x * x, axis=-1, keepdims=True) + RMS_EPS) * g_ref[...]


def _final_norm(x, g):
    rows = x.shape[0]
    tm = 256
    return pl.pallas_call(
        _final_norm_kernel,
        grid=(rows // tm,),
        in_specs=[pl.BlockSpec((tm, D_MODEL), lambda i: (i, 0)),
                  pl.BlockSpec((1, D_MODEL), lambda i: (0, 0))],
        out_specs=pl.BlockSpec((tm, D_MODEL), lambda i: (i, 0)),
        out_shape=jax.ShapeDtypeStruct((rows, D_MODEL), F32),
        compiler_params=_cparams("parallel"),
    )(x, g.reshape(1, D_MODEL))


def _mm_kernel(a_ref, w_ref, o_ref):
    o_ref[...] = jnp.dot(a_ref[...], w_ref[...].astype(BF16),
                         preferred_element_type=F32).astype(o_ref.dtype)


def _matmul(a, w, l, out_dtype, tm=1024, tn=512):
    m, k = a.shape
    n = w.shape[2]
    return pl.pallas_call(
        _mm_kernel,
        grid=(m // tm, n // tn),
        in_specs=[pl.BlockSpec((tm, k), lambda i, j: (i, 0)),
                  pl.BlockSpec((None, k, tn), lambda i, j: (l, 0, j))],
        out_specs=pl.BlockSpec((tm, tn), lambda i, j: (i, j)),
        out_shape=jax.ShapeDtypeStruct((m, n), out_dtype),
        compiler_params=_cparams("parallel", "parallel"),
    )(a, w)


NA_QROWS = 4
NA_BAND = NA_KH + NA_QROWS
NA_TQ = NA_QROWS * GRID_W
NA_TK = NA_BAND * GRID_W


def _na_bias_table(rel_bias):
    cidx = jnp.arange(GRID_W, dtype=I32)
    col_start = jnp.clip(cidx - NA_KW // 2, 0, GRID_W - NA_KW)
    col_ok = (cidx[None, :] >= col_start[:, None]) & (cidx[None, :] < col_start[:, None] + NA_KW)
    dc = jnp.clip(cidx[None, :] - cidx[:, None] + (NA_KW - 1), 0, 2 * NA_KW - 2)
    t = rel_bias.astype(F32)[:, :, dc]
    t = jnp.where(col_ok[None, None], t, NEG)
    pad = jnp.full((NA_HEADS, 1, GRID_W, GRID_W), NEG, F32)
    t = jnp.concatenate([pad, t, pad], axis=1)
    return jnp.concatenate([t[:, :-1], t[:, 1:]], axis=-1)


def _na_kernel(q_ref, k_ref, v_ref, kc_ref, vc_ref, pt_ref, o_ref):
    i = pl.program_id(2)
    start = jnp.clip(NA_QROWS * i - NA_KH // 2, 0, GRID_H - NA_BAND)
    koff = pl.multiple_of(start * GRID_W, GRID_W * NA_QROWS)
    q = q_ref[...].astype(BF16)
    kb = k_ref[pl.ds(koff, NA_TK), :].astype(BF16)
    vb = v_ref[pl.ds(koff, NA_TK), :].astype(BF16)
    nt = (((1,), (1,)), ((), ()))
    s = lax.dot_general(q, kb, nt, preferred_element_type=F32) * ATT_SCALE
    lane = lax.broadcasted_iota(I32, (GRID_W, 2 * GRID_W), 1)
    bias_rows = []
    for a in range(NA_QROWS):
        rq = NA_QROWS * i + a
        ws = jnp.clip(rq - NA_KH // 2, 0, GRID_H - NA_KH)
        tiles = []
        for jp in range(NA_BAND // 2):
            rk = start + 2 * jp
            e = jnp.clip(rk - rq + NA_KH, 0, 2 * NA_KH - 1)
            ok0 = ((rk >= ws) & (rk < ws + NA_KH)).astype(I32)
            ok1 = ((rk + 1 >= ws) & (rk + 1 < ws + NA_KH)).astype(I32)
            ok = jnp.where(lane < GRID_W, ok0, ok1) > 0
            tiles.append(jnp.where(ok, pt_ref[e], NEG))
        bias_rows.append(jnp.concatenate(tiles, axis=1))
    s = s + jnp.concatenate(bias_rows, axis=0)
    sc = lax.dot_general(q, kc_ref[...].astype(BF16), nt, preferred_element_type=F32) * ATT_SCALE
    m = jnp.maximum(jnp.max(s, axis=-1, keepdims=True), jnp.max(sc, axis=-1, keepdims=True))
    p = jnp.exp(s - m)
    pc = jnp.exp(sc - m)
    den = jnp.sum(p, axis=-1, keepdims=True) + jnp.sum(pc, axis=-1, keepdims=True)
    o = (jnp.dot(p.astype(BF16), vb, preferred_element_type=F32)
         + jnp.dot(pc.astype(BF16), vc_ref[...].astype(BF16), preferred_element_type=F32))
    o_ref[...] = (o / den).astype(o_ref.dtype)


def _na_attention(z, pt):
    qb = SEQ // NA_TQ
    cq, ck, cv = COL_QA // HEAD_DIM, COL_KA // HEAD_DIM, COL_VA // HEAD_DIM
    ctx_blk = N_LAT // CTX_LEN
    return pl.pallas_call(
        _na_kernel,
        grid=(BATCH, NA_HEADS, qb),
        in_specs=[
            pl.BlockSpec((NA_TQ, HEAD_DIM), lambda b, h, i: (b * qb + i, cq + h)),
            pl.BlockSpec((SEQ, HEAD_DIM), lambda b, h, i: (b, ck + h)),
            pl.BlockSpec((SEQ, HEAD_DIM), lambda b, h, i: (b, cv + h)),
            pl.BlockSpec((CTX_LEN, HEAD_DIM), lambda b, h, i: (ctx_blk + b, ck + h)),
            pl.BlockSpec((CTX_LEN, HEAD_DIM), lambda b, h, i: (ctx_blk + b, cv + h)),
            pl.BlockSpec((None, 2 * NA_KH, GRID_W, 2 * GRID_W), lambda b, h, i: (h, 0, 0, 0)),
        ],
        out_specs=pl.BlockSpec((NA_TQ, HEAD_DIM), lambda b, h, i: (b * qb + i, h)),
        out_shape=jax.ShapeDtypeStruct((N_LAT, NA_WIDTH), BF16),
        compiler_params=_cparams("parallel", "parallel", "parallel"),
    )(z, z, z, z, z, pt)


def _softmax_attend(q, ks, vs):
    nt = (((1,), (1,)), ((), ()))
    ss = [lax.dot_general(q, k, nt, preferred_element_type=F32) * ATT_SCALE for k in ks]
    m = ss[0].max(axis=-1, keepdims=True)
    for s in ss[1:]:
        m = jnp.maximum(m, s.max(axis=-1, keepdims=True))
    ps = [jnp.exp(s - m) for s in ss]
    den = ps[0].sum(axis=-1, keepdims=True)
    for p in ps[1:]:
        den = den + p.sum(axis=-1, keepdims=True)
    o = jnp.dot(ps[0].astype(BF16), vs[0], preferred_element_type=F32)
    for p, v in zip(ps[1:], vs[1:]):
        o = o + jnp.dot(p.astype(BF16), v, preferred_element_type=F32)
    return o / den


def _stack_heads(q, group):
    return jnp.concatenate([q[:, g * HEAD_DIM:(g + 1) * HEAD_DIM] for g in range(group)], axis=0)


def _unstack_heads(o, group, rows):
    return jnp.concatenate([o[g * rows:(g + 1) * rows] for g in range(group)], axis=1)


def _ctx_attn_kernel(q_ref, k_ref, v_ref, o_ref, *, group):
    q = _stack_heads(q_ref[...].astype(BF16), group)
    o = _softmax_attend(q, [k_ref[...].astype(BF16)], [v_ref[...].astype(BF16)])
    o_ref[...] = _unstack_heads(o, group, CTX_LEN).astype(o_ref.dtype)


def _ctx_attention(q_arr, q_col, k_arr, k_col, v_arr, v_col, kv_heads, group):
    ctx_blk = N_LAT // CTX_LEN
    qw = group * HEAD_DIM
    return pl.pallas_call(
        functools.partial(_ctx_attn_kernel, group=group),
        grid=(BATCH, kv_heads),
        in_specs=[
            pl.BlockSpec((CTX_LEN, qw), lambda b, h: (ctx_blk + b, q_col // group + h)),
            pl.BlockSpec((CTX_LEN, HEAD_DIM), lambda b, h: (ctx_blk + b, k_col + h)),
            pl.BlockSpec((CTX_LEN, HEAD_DIM), lambda b, h: (ctx_blk + b, v_col + h)),
        ],
        out_specs=pl.BlockSpec((CTX_LEN, qw), lambda b, h: (b, h)),
        out_shape=jax.ShapeDtypeStruct((N_CTX, kv_heads * qw), BF16),
        compiler_params=_cparams("parallel", "parallel"),
    )(q_arr, k_arr, v_arr)


def _rope_tables():
    t = jnp.arange(SEQ, dtype=I32)
    row = (t // GRID_W).astype(F32)
    col = (t % GRID_W).astype(F32)
    n_freq = HEAD_DIM // 4
    inv_freq = ROPE_THETA ** (-jnp.arange(n_freq, dtype=F32) / n_freq)
    ar = row[:, None] * inv_freq
    ac = col[:, None] * inv_freq
    cos = jnp.concatenate([jnp.cos(ar), jnp.cos(ar), jnp.cos(ac), jnp.cos(ac)], axis=-1)
    sin = jnp.concatenate([-jnp.sin(ar), jnp.sin(ar), -jnp.sin(ac), jnp.sin(ac)], axis=-1)
    cos = jnp.concatenate([cos, jnp.ones((CTX_LEN, HEAD_DIM), F32)], axis=0)
    sin = jnp.concatenate([sin, jnp.zeros((CTX_LEN, HEAD_DIM), F32)], axis=0)
    return cos, sin


def _qkprep_kernel(q0_ref, q1_ref, k_ref, cos_ref, sin_ref, qg_ref, kg_ref, o_ref):
    c = cos_ref[...]
    s = sin_ref[...]
    lane = lax.broadcasted_iota(I32, c.shape, 1)
    first = (lane % (HEAD_DIM // 2)) < (HEAD_DIM // 4)

    def prep(x, g):
        y = x * lax.rsqrt(jnp.mean(x * x, axis=-1, keepdims=True) + RMS_EPS) * g
        partner = jnp.where(first, pltpu.roll(y, HEAD_DIM - HEAD_DIM // 4, 1),
                            pltpu.roll(y, HEAD_DIM // 4, 1))
        return (y * c + partner * s).astype(o_ref.dtype)

    half = GQA_HEADS // 2
    for h in range(half):
        sl = slice(h * HEAD_DIM, (h + 1) * HEAD_DIM)
        o_ref[:, sl] = prep(q0_ref[:, sl], qg_ref[...])
        o_ref[:, (half + h) * HEAD_DIM:(half + h + 1) * HEAD_DIM] = prep(q1_ref[:, sl], qg_ref[...])
    for h in range(GQA_KV_HEADS):
        sl = slice(h * HEAD_DIM, (h + 1) * HEAD_DIM)
        o_ref[:, (GQA_HEADS + h) * HEAD_DIM:(GQA_HEADS + h + 1) * HEAD_DIM] = prep(k_ref[:, sl], kg_ref[...])


def _qk_prep(z, q_norm_g, k_norm_g, cos, sin):
    tm = 256
    hq = GQA_Q_WIDTH // 2
    lat_blocks = N_LAT // tm
    per_batch = SEQ // tm
    tbl = lambda i: (jnp.where(i < lat_blocks, i % per_batch, per_batch), 0)
    return pl.pallas_call(
        _qkprep_kernel,
        grid=(N_ALL // tm,),
        in_specs=[
            pl.BlockSpec((tm, hq), lambda i: (i, COL_QG // hq)),
            pl.BlockSpec((tm, hq), lambda i: (i, COL_QG // hq + 1)),
            pl.BlockSpec((tm, GQA_KV_WIDTH), lambda i: (i, COL_KG // GQA_KV_WIDTH)),
            pl.BlockSpec((tm, HEAD_DIM), tbl),
            pl.BlockSpec((tm, HEAD_DIM), tbl),
            pl.BlockSpec((1, HEAD_DIM), lambda i: (0, 0)),
            pl.BlockSpec((1, HEAD_DIM), lambda i: (0, 0)),
        ],
        out_specs=pl.BlockSpec((tm, GQA_Q_WIDTH + GQA_KV_WIDTH), lambda i: (i, 0)),
        out_shape=jax.ShapeDtypeStruct((N_ALL, GQA_Q_WIDTH + GQA_KV_WIDTH), BF16),
        compiler_params=_cparams("parallel"),
    )(z, z, z, cos, sin, q_norm_g.reshape(1, HEAD_DIM), k_norm_g.reshape(1, HEAD_DIM))


GQA_TQ = 256


def _gqa_kernel(q_ref, k_ref, kc_ref, v_ref, vc_ref, o_ref):
    q = _stack_heads(q_ref[...], GQA_GROUP)
    o = _softmax_attend(q, [k_ref[...], kc_ref[...]],
                        [v_ref[...].astype(BF16), vc_ref[...].astype(BF16)])
    o_ref[...] = _unstack_heads(o, GQA_GROUP, GQA_TQ).astype(o_ref.dtype)


def _gqa_attention(qk, z):
    qb = SEQ // GQA_TQ
    qw = GQA_GROUP * HEAD_DIM
    kcol = GQA_Q_WIDTH // HEAD_DIM
    vcol = COL_VG // HEAD_DIM
    ctx_blk = N_LAT // CTX_LEN
    return pl.pallas_call(
        _gqa_kernel,
        grid=(BATCH, GQA_KV_HEADS, qb),
        in_specs=[
            pl.BlockSpec((GQA_TQ, qw), lambda b, h, i: (b * qb + i, h)),
            pl.BlockSpec((SEQ, HEAD_DIM), lambda b, h, i: (b, kcol + h)),
            pl.BlockSpec((CTX_LEN, HEAD_DIM), lambda b, h, i: (ctx_blk + b, kcol + h)),
            pl.BlockSpec((SEQ, HEAD_DIM), lambda b, h, i: (b, vcol + h)),
            pl.BlockSpec((CTX_LEN, HEAD_DIM), lambda b, h, i: (ctx_blk + b, vcol + h)),
        ],
        out_specs=pl.BlockSpec((GQA_TQ, qw), lambda b, h, i: (b * qb + i, h)),
        out_shape=jax.ShapeDtypeStruct((N_LAT, GQA_Q_WIDTH), BF16),
        compiler_params=_cparams("parallel", "parallel", "parallel"),
    )(qk, qk, qk, z, z)


def _pair_blockdiag(x):
    g, a, b = x.shape
    x2 = x.reshape(g // 2, 2, a, b)
    eye = jnp.eye(2, dtype=x.dtype)
    return jnp.einsum('qiab,ij->qiajb', x2, eye).reshape(g // 2, 2 * a, 2 * b)


def _s5_weights(lam_re, lam_im, log_dt, b_re, b_im, c_re, c_im, d_skip):
    hi = lax.Precision.HIGHEST
    G, N, P, C = S5_GROUPS, S5_STATE, S5_GROUP_CH, S5_CHUNK
    lam_re, lam_im = lam_re.astype(F32), lam_im.astype(F32)
    dt = jnp.exp(log_dt.astype(F32))[..., None]
    tau = jnp.arange(C + 1, dtype=F32)[None, None, :, None]
    mag = jnp.exp(lam_re[:, :, None, :] * dt[:, :, None, :] * tau)
    ang = lam_im[:, :, None, :] * dt[:, :, None, :] * tau
    lp_re, lp_im = mag * jnp.cos(ang), mag * jnp.sin(ang)
    nr, ni = lp_re[:, :, 1] - 1.0, lp_im[:, :, 1]
    den = lam_re * lam_re + lam_im * lam_im
    fr = (nr * lam_re + ni * lam_im) / den
    fi = (ni * lam_re - nr * lam_im) / den
    bb_re = fr[..., None] * b_re - fi[..., None] * b_im
    bb_im = fr[..., None] * b_im + fi[..., None] * b_re
    c_re, c_im = c_re.astype(F32), c_im.astype(F32)
    lb_re = lp_re[..., None] * bb_re[:, :, None] - lp_im[..., None] * bb_im[:, :, None]
    lb_im = lp_re[..., None] * bb_im[:, :, None] + lp_im[..., None] * bb_re[:, :, None]
    kern = (jnp.einsum('dgon,dgtni->dgtoi', c_re, lb_re[:, :, :C], precision=hi)
            - jnp.einsum('dgon,dgtni->dgtoi', c_im, lb_im[:, :, :C], precision=hi))
    s_idx = jnp.arange(C)[:, None]
    t_idx = jnp.arange(C)[None, :]
    kf = kern[0][:, jnp.clip(t_idx - s_idx, 0, C - 1)]
    kr = kern[1][:, jnp.clip(s_idx - t_idx, 0, C - 1)]
    kf = jnp.where((t_idx >= s_idx)[None, :, :, None, None], kf, 0.0)
    kr = jnp.where((s_idx >= t_idx)[None, :, :, None, None], kr, 0.0)
    skip = (jnp.eye(C, dtype=F32)[None, :, :, None, None]
            * (jnp.eye(P, dtype=F32)[None, None, None] * d_skip.astype(F32).reshape(G, 1, 1, P, 1)))
    m = (kf + kr + skip).transpose(0, 1, 4, 2, 3).reshape(G, C * P, C * P)
    wsf_re = lb_re[0][:, C - 1::-1][:, :C].transpose(0, 1, 3, 2).reshape(G, C * P, N)
    wsf_im = lb_im[0][:, C - 1::-1][:, :C].transpose(0, 1, 3, 2).reshape(G, C * P, N)
    wsr_re = lb_re[1][:, :C].transpose(0, 1, 3, 2).reshape(G, C * P, N)
    wsr_im = lb_im[1][:, :C].transpose(0, 1, 3, 2).reshape(G, C * P, N)
    def state_out(d, powers_re, powers_im):
        cr, ci = c_re[d][:, None], c_im[d][:, None]
        pr, pi_ = powers_re[:, :, None], powers_im[:, :, None]
        from_re = (cr * pr - ci * pi_).transpose(0, 3, 1, 2).reshape(G, N, C * P)
        from_im = (-(cr * pi_ + ci * pr)).transpose(0, 3, 1, 2).reshape(G, N, C * P)
        return from_re, from_im
    of_re, of_im = state_out(0, lp_re[0][:, 1:C + 1], lp_im[0][:, 1:C + 1])
    or_re, or_im = state_out(1, lp_re[1][:, C:0:-1], lp_im[1][:, C:0:-1])
    w_state = jnp.concatenate([_pair_blockdiag(w) for w in (wsf_re, wsf_im, wsr_re, wsr_im)], axis=-1)
    w_out = jnp.concatenate([_pair_blockdiag(w) for w in (m, of_re, of_im, or_re, or_im)], axis=1)
    lam_c = jnp.stack([lp_re[0][:, C], lp_im[0][:, C], lp_re[1][:, C], lp_im[1][:, C]])
    return w_state.astype(BF16), w_out.astype(BF16), lam_c.reshape(4, 1, S5_LANES)


def _s5_state_kernel(u_ref, w_ref, fre_ref, fim_ref, rre_ref, rim_ref):
    r = jnp.dot(u_ref[...], w_ref[...], preferred_element_type=F32)
    for k, ref in enumerate((fre_ref, fim_ref, rre_ref, rim_ref)):
        ref[...] = r[:, k * LANES:(k + 1) * LANES]


def _s5_local_states(u2, w_state):
    uw = 2 * S5_CHUNK * S5_GROUP_CH
    out = jax.ShapeDtypeStruct((S5_ROWS, S5_LANES), F32)
    ospec = pl.BlockSpec((S5_ROWS, LANES), lambda q: (0, q))
    return pl.pallas_call(
        _s5_state_kernel,
        grid=(S5_UNITS,),
        in_specs=[pl.BlockSpec((None, S5_ROWS, uw), lambda q: (q, 0, 0)),
                  pl.BlockSpec((None, uw, 4 * LANES), lambda q: (q, 0, 0))],
        out_specs=[ospec] * 4,
        out_shape=[out] * 4,
        compiler_params=_cparams("parallel"),
    )(u2, w_state)


S5_SCAN_LANES = 512
S5_PAIR_ROWS = 2 * BATCH


def _s5_scan_kernel(fre_ref, fim_ref, rre_ref, rim_ref, lam_ref,
                    pfre_ref, pfim_ref, prre_ref, prim_ref):
    lam = lam_ref[...]
    shape = (S5_PAIR_ROWS, S5_SCAN_LANES)
    top = lax.broadcasted_iota(I32, shape, 0) < BATCH
    n_pairs = S5_NCHUNK // 2
    ctx_pairs = S5_CTX_CHUNKS // 2

    def cmul_add(ar, ai, sr, si, lr, li):
        return ar * sr - ai * si + lr, ar * si + ai * sr + li

    def half_step(ar, ai, sr, si, lr, li, first_top):
        xr, xi = cmul_add(ar, ai, sr, si, lr, li)
        xr, xi = pltpu.roll(xr, BATCH, 0), pltpu.roll(xi, BATCH, 0)
        keep = top if first_top else ~top
        prev_r, prev_i = jnp.where(keep, sr, xr), jnp.where(keep, si, xi)
        yr, yi = cmul_add(ar, ai, xr, xi, lr, li)
        nr = jnp.where(keep, pltpu.roll(yr, BATCH, 0), yr)
        ni = jnp.where(keep, pltpu.roll(yi, BATCH, 0), yi)
        return prev_r, prev_i, nr, ni

    def body(kk, carry):
        sfr, sfi, srr, sri = carry
        fo = pl.multiple_of(kk * S5_PAIR_ROWS, S5_PAIR_ROWS)
        rp = jnp.where(kk < ctx_pairs, ctx_pairs - 1 - kk, n_pairs - 1 + ctx_pairs - kk)
        ro = pl.multiple_of(rp * S5_PAIR_ROWS, S5_PAIR_ROWS)
        pr, pi_, sfr, sfi = half_step(lam[0], lam[1], sfr, sfi,
                                      fre_ref[pl.ds(fo, S5_PAIR_ROWS), :],
                                      fim_ref[pl.ds(fo, S5_PAIR_ROWS), :], True)
        pfre_ref[pl.ds(fo, S5_PAIR_ROWS), :] = pr
        pfim_ref[pl.ds(fo, S5_PAIR_ROWS), :] = pi_
        pr, pi_, srr, sri = half_step(lam[2], lam[3], srr, sri,
                                      rre_ref[pl.ds(ro, S5_PAIR_ROWS), :],
                                      rim_ref[pl.ds(ro, S5_PAIR_ROWS), :], False)
        prre_ref[pl.ds(ro, S5_PAIR_ROWS), :] = pr
        prim_ref[pl.ds(ro, S5_PAIR_ROWS), :] = pi_
        return sfr, sfi, srr, sri

    zero = jnp.zeros(shape, F32)
    lax.fori_loop(0, n_pairs, body, (zero, zero, zero, zero))


def _s5_chunk_scan(loc, lam_c):
    spec = pl.BlockSpec((S5_ROWS, S5_SCAN_LANES), lambda j: (0, j))
    out = jax.ShapeDtypeStruct((S5_ROWS, S5_LANES), F32)
    return pl.pallas_call(
        _s5_scan_kernel,
        grid=(S5_LANES // S5_SCAN_LANES,),
        in_specs=[spec] * 4 + [pl.BlockSpec((4, 1, S5_SCAN_LANES), lambda j: (0, 0, j))],
        out_specs=[spec] * 4,
        out_shape=[out] * 4,
        compiler_params=_cparams("parallel"),
    )(*loc, lam_c)


def _s5_out_kernel(u_ref, fre_ref, fim_ref, rre_ref, rim_ref, w_ref, o_ref):
    lhs = jnp.concatenate([u_ref[...]] + [r[...].astype(BF16)
                                           for r in (fre_ref, fim_ref, rre_ref, rim_ref)], axis=1)
    o_ref[...] = jnp.dot(lhs, w_ref[...], preferred_element_type=F32)


def _s5_outputs(u2, prev, w_out):
    uw = 2 * S5_CHUNK * S5_GROUP_CH
    sspec = pl.BlockSpec((S5_ROWS, LANES), lambda q: (0, q))
    return pl.pallas_call(
        _s5_out_kernel,
        grid=(S5_UNITS,),
        in_specs=[pl.BlockSpec((None, S5_ROWS, uw), lambda q: (q, 0, 0))] + [sspec] * 4
                 + [pl.BlockSpec((None, uw + 4 * LANES, uw), lambda q: (q, 0, 0))],
        out_specs=pl.BlockSpec((None, S5_ROWS, uw), lambda q: (q, 0, 0)),
        out_shape=jax.ShapeDtypeStruct((S5_UNITS, S5_ROWS, uw), F32),
        compiler_params=_cparams("parallel"),
    )(u2, *prev, w_out)


def _glu_kernel(y_ref, w_ref, o_ref):
    y = y_ref[...]
    g = 0.5 * y * (1.0 + jnp.tanh(math.sqrt(2.0 / math.pi) * (y + 0.044715 * (y * y * y))))
    t = jnp.dot(g.astype(BF16), w_ref[...].astype(BF16), preferred_element_type=F32)
    o_ref[...] = (g * _sigmoid(t)).astype(o_ref.dtype)


def _s5_glu(y, w_glu, l):
    rows = y.shape[0]
    tm = 512
    return pl.pallas_call(
        _glu_kernel,
        grid=(rows // tm,),
        in_specs=[pl.BlockSpec((tm, S5_WIDTH), lambda i: (i, 0)),
                  pl.BlockSpec((None, S5_WIDTH, S5_WIDTH), lambda i: (l, 0, 0))],
        out_specs=pl.BlockSpec((tm, S5_WIDTH), lambda i: (i, 0)),
        out_shape=jax.ShapeDtypeStruct((rows, S5_WIDTH), BF16),
        compiler_params=_cparams("parallel"),
    )(y, w_glu)


def _s5_branch(z, s5w, w_glu, l):
    w_state, w_out, lam_c = s5w
    C, P = S5_CHUNK, S5_GROUP_CH
    u = lax.optimization_barrier(z[:, COL_U:COL_U + S5_WIDTH])
    u_seq = jnp.concatenate([u[N_LAT:].reshape(BATCH, CTX_LEN, S5_WIDTH),
                             u[:N_LAT].reshape(BATCH, SEQ, S5_WIDTH)], axis=1)
    u2 = (u_seq.astype(BF16).reshape(BATCH, S5_NCHUNK, C, S5_UNITS, 2, P)
          .transpose(3, 1, 0, 4, 2, 5).reshape(S5_UNITS, S5_ROWS, 2 * C * P))
    loc = _s5_local_states(u2, w_state)
    prev = _s5_chunk_scan(loc, lam_c)
    y2 = _s5_outputs(u2, prev, w_out)
    y = (y2.reshape(S5_UNITS, S5_NCHUNK, BATCH, 2, C, P)
         .transpose(2, 1, 4, 0, 3, 5).reshape(BATCH, S5_SEQ, S5_WIDTH))
    y_all = jnp.concatenate([y[:, CTX_LEN:].reshape(N_LAT, S5_WIDTH),
                             y[:, :CTX_LEN].reshape(N_CTX, S5_WIDTH)], axis=0)
    return _s5_glu(y_all, w_glu, l)


def _merge_kernel(ya_ref, ys_ref, yc_ref, wa_ref, ws_ref, wc_ref, ga_ref, gs_ref, gc_ref, o_ref):
    def branch(y_ref, w_ref, g_ref):
        return _sigmoid(g_ref[...]) * jnp.dot(y_ref[...], w_ref[...].astype(BF16),
                                              preferred_element_type=F32)
    o_ref[...] = (branch(ya_ref, wa_ref, ga_ref) + branch(ys_ref, ws_ref, gs_ref)
                  + branch(yc_ref, wc_ref, gc_ref)).astype(o_ref.dtype)


def _gated_merge(ya, ys, yc, z, w_br_a, w_br_s, w_br_c, l):
    rows = ya.shape[0]
    tm, tn = 512, 512
    aspec = pl.BlockSpec((tm, NA_WIDTH), lambda i, j: (i, 0))
    wspec = pl.BlockSpec((None, NA_WIDTH, tn), lambda i, j: (l, 0, j))
    gspec = lambda col: pl.BlockSpec((tm, tn), lambda i, j: (i, col // tn + j))
    return pl.pallas_call(
        _merge_kernel,
        grid=(rows // tm, D_MODEL // tn),
        in_specs=[aspec] * 3 + [wspec] * 3 + [gspec(COL_GA), gspec(COL_GS), gspec(COL_GC)],
        out_specs=pl.BlockSpec((tm, tn), lambda i, j: (i, j)),
        out_shape=jax.ShapeDtypeStruct((rows, D_MODEL), BF16),
        compiler_params=_cparams("parallel", "parallel"),
    )(ya, ys, yc, w_br_a, w_br_s, w_br_c, z, z, z)


def _proj_res_kernel(m_ref, w_ref, x_ref, g_ref, o_ref):
    y = jnp.dot(m_ref[...], w_ref[...].astype(BF16), preferred_element_type=F32)
    o_ref[...] = x_ref[...] + g_ref[...] * y


def _proj_residual(m, w_out, l, x, mod3, gate_idx):
    rows = m.shape[0]
    tm, tn = 1024, 512
    nj = D_MODEL // tn
    return pl.pallas_call(
        _proj_res_kernel,
        grid=(rows // tm, nj),
        in_specs=[
            pl.BlockSpec((tm, D_MODEL), lambda i, j: (i, 0)),
            pl.BlockSpec((None, D_MODEL, tn), lambda i, j: (l, 0, j)),
            pl.BlockSpec((tm, tn), lambda i, j: (i, j)),
            pl.BlockSpec((None, 1, tn), lambda i, j: (_batch_of_block(i, tm), 0, gate_idx * nj + j)),
        ],
        out_specs=pl.BlockSpec((tm, tn), lambda i, j: (i, j)),
        out_shape=jax.ShapeDtypeStruct((rows, D_MODEL), F32),
        compiler_params=_cparams("parallel", "parallel"),
    )(m, w_out, x, mod3)


def _router_kernel(x_ref, g_ref, sc_ref, sh_ref, wr_ref, br_ref, h_ref, idx_ref, wt_ref, rank_ref,
                   cnt_ref, count):
    @pl.when(pl.program_id(0) == 0)
    def _():
        count[...] = jnp.zeros(count.shape, count.dtype)

    x = x_ref[...]
    tm = x.shape[0]
    y = x * lax.rsqrt(jnp.mean(x * x, axis=-1, keepdims=True) + RMS_EPS) * g_ref[...]
    h = y * (1.0 + sc_ref[...]) + sh_ref[...]
    for k in range(ROW_CHUNKS):
        h_ref[pl.ds(k, tm, stride=ROW_CHUNKS), :] = h[:, k * LANES:(k + 1) * LANES]
    logits = jnp.dot(h, wr_ref[...], preferred_element_type=F32,
                     precision=lax.Precision.HIGHEST) + br_ref[...]
    lane = lax.broadcasted_iota(I32, logits.shape, 1)
    vals, idxs = [], []
    for _ in range(TOP_K):
        m = jnp.max(logits, axis=-1, keepdims=True)
        sel = jnp.min(jnp.where(logits == m, lane, LANES), axis=-1, keepdims=True)
        vals.append(m)
        idxs.append(sel)
        logits = jnp.where(lane == sel, NEG, logits)
    es = [jnp.exp(v - vals[0]) for v in vals]
    den = es[0] + es[1] + es[2] + es[3]
    hits = [lane == idxs[k] for k in range(TOP_K)]
    per_expert = hits[0].astype(F32)
    for k in range(1, TOP_K):
        per_expert = per_expert + hits[k].astype(F32)
    tri = (lax.broadcasted_iota(I32, (tm, tm), 1) < lax.broadcasted_iota(I32, (tm, tm), 0)).astype(BF16)
    before = jnp.dot(tri, per_expert.astype(BF16), preferred_element_type=F32) + count[...]
    idx_out = jnp.zeros(lane.shape, I32)
    wt_out = jnp.zeros(lane.shape, F32)
    rank_out = jnp.zeros(lane.shape, I32)
    for k in range(TOP_K):
        rank_k = jnp.sum(jnp.where(hits[k], before, 0.0), axis=-1, keepdims=True).astype(I32)
        idx_out = jnp.where(lane == k, idxs[k], idx_out)
        wt_out = jnp.where(lane == k, es[k] / den, wt_out)
        rank_out = jnp.where(lane == k, rank_k, rank_out)
    idx_ref[...] = idx_out
    wt_ref[...] = wt_out
    rank_ref[...] = rank_out
    count[...] = count[...] + jnp.sum(per_expert, axis=0, keepdims=True)
    cnt_ref[...] = count[...]


def _norm_router(x, g, mod3, shift_idx, scale_idx, w_router, b_router):
    rows = x.shape[0]
    tm = 256
    wr = jnp.zeros((D_MODEL, LANES), F32).at[:, :N_EXPERTS].set(w_router)
    br = jnp.full((1, LANES), NEG, F32).at[0, :N_EXPERTS].set(b_router)
    small = pl.BlockSpec((tm, LANES), lambda i: (i, 0))
    return pl.pallas_call(
        _router_kernel,
        grid=(rows // tm,),
        in_specs=[
            pl.BlockSpec((tm, D_MODEL), lambda i: (i, 0)),
            pl.BlockSpec((1, D_MODEL), lambda i: (0, 0)),
            pl.BlockSpec((None, 1, D_MODEL), lambda i: (_batch_of_block(i, tm), 0, scale_idx)),
            pl.BlockSpec((None, 1, D_MODEL), lambda i: (_batch_of_block(i, tm), 0, shift_idx)),
            pl.BlockSpec((D_MODEL, LANES), lambda i: (0, 0)),
            pl.BlockSpec((1, LANES), lambda i: (0, 0)),
        ],
        out_specs=[pl.BlockSpec((tm * ROW_CHUNKS, LANES), lambda i: (i, 0)), small, small, small,
                   pl.BlockSpec((1, LANES), lambda i: (0, 0))],
        out_shape=[jax.ShapeDtypeStruct((rows * ROW_CHUNKS, LANES), F32),
                   jax.ShapeDtypeStruct((rows, LANES), I32),
                   jax.ShapeDtypeStruct((rows, LANES), F32),
                   jax.ShapeDtypeStruct((rows, LANES), I32),
                   jax.ShapeDtypeStruct((1, LANES), F32)],
        scratch_shapes=[pltpu.VMEM((1, LANES), F32)],
        compiler_params=_cparams("arbitrary"),
    )(x, g.reshape(1, D_MODEL), mod3, mod3, wr, br)


W2_PREP_ROWS = 256


def _w2_prep_kernel(w_ref, o_ref, scr):
    half = LANES // 2
    for k in range(ROW_CHUNKS):
        cols = slice(k * LANES, (k + 1) * LANES)
        s = scr.at[k]
        for t in range(W2_PREP_ROWS // LANES):
            s[pl.ds(t * LANES, half, stride=2), :] = w_ref[t * LANES:t * LANES + half, cols]
            s[pl.ds(t * LANES + 1, half, stride=2), :] = w_ref[t * LANES + half:(t + 1) * LANES, cols]
        o_ref[:, cols] = s[...].astype(o_ref.dtype)


def _w2_prep(w2):
    de = DEPTH * N_EXPERTS
    spec = pl.BlockSpec((None, W2_PREP_ROWS, D_MODEL), lambda e, r: (e, r, 0))
    out = pl.pallas_call(
        _w2_prep_kernel,
        grid=(de, D_FF_EXPERT // W2_PREP_ROWS),
        in_specs=[spec],
        out_specs=spec,
        out_shape=jax.ShapeDtypeStruct((de, D_FF_EXPERT, D_MODEL), BF16),
        scratch_shapes=[pltpu.VMEM((ROW_CHUNKS, W2_PREP_ROWS, LANES), F32)],
        compiler_params=_cparams("parallel", "parallel"),
    )(w2.reshape(de, D_FF_EXPERT, D_MODEL))
    return out.reshape(DEPTH, N_EXPERTS, D_FF_EXPERT, D_MODEL)


def _expert_kernel(be_ref, tok_ref, nused_ref, h_hbm, w1_ref, b1_ref, w2_ref, b2_ref,
                   o_ref, gbuf, xb, sem):
    i = pl.program_id(0)
    n_used = nused_ref[0]
    slot = i % 2

    def start_gather(blk, s):
        base = blk * MOE_BLOCK

        def issue(j, c):
            pltpu.make_async_copy(h_hbm.at[tok_ref[base + j]],
                                  gbuf.at[s, pl.ds(j * ROW_CHUNKS, ROW_CHUNKS), :], sem.at[s]).start()
            return c
        lax.fori_loop(0, MOE_BLOCK, issue, 0, unroll=8)

    def wait_gather(s):
        def wait(j, c):
            pltpu.make_async_copy(h_hbm.at[0], gbuf.at[s, pl.ds(j * ROW_CHUNKS, ROW_CHUNKS), :],
                                  sem.at[s]).wait()
            return c
        lax.fori_loop(0, MOE_BLOCK, wait, 0, unroll=8)

    @pl.when(i == 0)
    def _():
        start_gather(0, 0)

    @pl.when(i + 1 < n_used)
    def _():
        start_gather(i + 1, 1 - slot)

    @pl.when(i < n_used)
    def _():
        wait_gather(slot)
        g = gbuf.at[slot]
        for k in range(ROW_CHUNKS):
            xb[:, k * LANES:(k + 1) * LANES] = g[pl.ds(k, MOE_BLOCK, stride=ROW_CHUNKS), :].astype(BF16)
        a = jnp.dot(xb[...], w1_ref[...], preferred_element_type=F32) + b1_ref[...]
        even = lax.broadcasted_iota(I32, (MOE_BLOCK, LANES), 1) % 2 == 0
        prods = []
        for s in range(2 * D_FF_EXPERT // LANES):
            a_s = a[:, s * LANES:(s + 1) * LANES]
            gate = jnp.minimum(a_s, SWIGLU_LIMIT)
            lin = jnp.clip(a_s, -SWIGLU_LIMIT, SWIGLU_LIMIT) + 1.0
            prods.append(gate * _sigmoid(SWIGLU_ALPHA * gate) * pltpu.roll(lin, LANES - 1, 1))
        acts = [jnp.where(even, prods[2 * t], pltpu.roll(prods[2 * t + 1], 1, 1))
                for t in range(D_FF_EXPERT // LANES)]
        act = jnp.concatenate(acts, axis=1).astype(BF16)
        y = jnp.dot(act, w2_ref[...], preferred_element_type=F32) + b2_ref[...]
        for k in range(ROW_CHUNKS):
            o_ref[pl.ds(k, MOE_BLOCK, stride=ROW_CHUNKS), :] = y[:, k * LANES:(k + 1) * LANES]

    @pl.when(i >= n_used)
    def _():
        o_ref[...] = jnp.zeros(o_ref.shape, o_ref.dtype)


def _expert_ffn(h_rows, block_expert, slot_token, n_used, w1, b1, w2, b2, l):
    nb = block_expert.shape[0]
    rows = h_rows.shape[0] // ROW_CHUNKS
    grid_spec = pltpu.PrefetchScalarGridSpec(
        num_scalar_prefetch=3,
        grid=(nb,),
        in_specs=[
            pl.BlockSpec(memory_space=pl.ANY),
            pl.BlockSpec((None, None, D_MODEL, 2 * D_FF_EXPERT), lambda i, be, tok, nu: (l, be[i], 0, 0)),
            pl.BlockSpec((None, None, 1, 2 * D_FF_EXPERT), lambda i, be, tok, nu: (l, be[i], 0, 0)),
            pl.BlockSpec((None, None, D_FF_EXPERT, D_MODEL), lambda i, be, tok, nu: (l, be[i], 0, 0)),
            pl.BlockSpec((None, None, 1, D_MODEL), lambda i, be, tok, nu: (l, be[i], 0, 0)),
        ],
        out_specs=pl.BlockSpec((MOE_BLOCK * ROW_CHUNKS, LANES), lambda i, be, tok, nu: (i, 0)),
        scratch_shapes=[
            pltpu.VMEM((2, MOE_BLOCK * ROW_CHUNKS, LANES), F32),
            pltpu.VMEM((MOE_BLOCK, D_MODEL), BF16),
            pltpu.SemaphoreType.DMA((2,)),
        ],
    )
    return pl.pallas_call(
        _expert_kernel,
        grid_spec=grid_spec,
        out_shape=jax.ShapeDtypeStruct((nb * MOE_BLOCK * ROW_CHUNKS, LANES), F32),
        compiler_params=_cparams("arbitrary"),
    )(block_expert, slot_token, n_used, h_rows.reshape(rows, ROW_CHUNKS, LANES), w1, b1, w2, b2)


COMB_TM = 128


def _combine_kernel(src_ref, y_hbm, x_ref, g_ref, wt_ref, o_ref, cbuf, sem):
    i = pl.program_id(0)
    n = pl.num_programs(0)
    slot = i % 2
    n_rows = COMB_TM * TOP_K

    def start_gather(blk, s):
        base = blk * n_rows

        def issue(r, c):
            pltpu.make_async_copy(y_hbm.at[src_ref[base + r]],
                                  cbuf.at[s, pl.ds(r * ROW_CHUNKS, ROW_CHUNKS), :], sem.at[s]).start()
            return c
        lax.fori_loop(0, n_rows, issue, 0, unroll=8)

    @pl.when(i == 0)
    def _():
        start_gather(0, 0)

    @pl.when(i + 1 < n)
    def _():
        start_gather(i + 1, 1 - slot)

    def wait(r, c):
        pltpu.make_async_copy(y_hbm.at[0], cbuf.at[slot, pl.ds(r * ROW_CHUNKS, ROW_CHUNKS), :],
                              sem.at[slot]).wait()
        return c
    lax.fori_loop(0, n_rows, wait, 0, unroll=8)
    c = cbuf.at[slot]
    wt = wt_ref[...]
    wk = [jnp.broadcast_to(wt[:, kk:kk + 1], (COMB_TM, LANES)) for kk in range(TOP_K)]
    for k in range(ROW_CHUNKS):
        f = wk[0] * c[pl.ds(k, COMB_TM, stride=ROW_CHUNKS), :]
        for kk in range(1, TOP_K):
            f = f + wk[kk] * c[pl.ds(kk * COMB_TM * ROW_CHUNKS + k, COMB_TM, stride=ROW_CHUNKS), :]
        sl = slice(k * LANES, (k + 1) * LANES)
        o_ref[:, sl] = x_ref[:, sl] + g_ref[:, sl] * f


def _moe_combine(src, y_rows, wt_pad, x, mod3, gate_idx):
    rows = x.shape[0]
    n_slots = y_rows.shape[0] // ROW_CHUNKS
    grid_spec = pltpu.PrefetchScalarGridSpec(
        num_scalar_prefetch=1,
        grid=(rows // COMB_TM,),
        in_specs=[
            pl.BlockSpec(memory_space=pl.ANY),
            pl.BlockSpec((COMB_TM, D_MODEL), lambda i, d: (i, 0)),
            pl.BlockSpec((None, 1, D_MODEL), lambda i, d: (_batch_of_block(i, COMB_TM), 0, gate_idx)),
            pl.BlockSpec((COMB_TM, LANES), lambda i, d: (i, 0)),
        ],
        out_specs=pl.BlockSpec((COMB_TM, D_MODEL), lambda i, d: (i, 0)),
        scratch_shapes=[
            pltpu.VMEM((2, COMB_TM * TOP_K * ROW_CHUNKS, LANES), F32),
            pltpu.SemaphoreType.DMA((2,)),
        ],
    )
    return pl.pallas_call(
        _combine_kernel,
        grid_spec=grid_spec,
        out_shape=jax.ShapeDtypeStruct((rows, D_MODEL), F32),
        compiler_params=_cparams("arbitrary"),
    )(src, y_rows.reshape(n_slots, ROW_CHUNKS, LANES), x, mod3, wt_pad)


def _moe_ffn(x, g, mod3, w_router, b_router, w1, b1, w2, b2, l):
    rows = x.shape[0]
    n_assign = rows * TOP_K
    nb = n_assign // MOE_BLOCK + N_EXPERTS
    h_rows, idx_pad, wt_pad, rank_pad, cnt = _norm_router(x, g, mod3, 3, 4, w_router, b_router)
    experts = jnp.arange(N_EXPERTS, dtype=I32)
    counts = cnt[0, :N_EXPERTS].astype(I32)
    padded = (counts + MOE_BLOCK - 1) // MOE_BLOCK * MOE_BLOCK
    padded_end = jnp.cumsum(padded)
    padded_start = padded_end - padded
    idx = idx_pad[:, :TOP_K]
    start_of = jnp.sum(jnp.where(idx[..., None] == experts, padded_start, 0), axis=-1)
    dest = start_of + rank_pad[:, :TOP_K]
    block_start = jnp.arange(nb, dtype=I32) * MOE_BLOCK
    block_expert = jnp.minimum(jnp.sum((block_start[:, None] >= padded_end[None, :]).astype(I32), axis=1),
                               N_EXPERTS - 1)
    n_used = padded_end[-1:] // MOE_BLOCK
    slot_token = jnp.zeros((nb * MOE_BLOCK,), I32).at[dest.reshape(-1)].set(
        jnp.arange(n_assign, dtype=I32) // TOP_K)
    y_rows = _expert_ffn(h_rows, block_expert, slot_token, n_used, w1, b1, w2, b2, l)
    src = dest.reshape(rows // COMB_TM, COMB_TM, TOP_K).transpose(0, 2, 1).reshape(-1)
    return _moe_combine(src, y_rows, wt_pad, x, mod3, 5)


def kernel(x, c, ctx, c_ctx, norm1_g, norm2_g, w_ada, b_ada, w_in, na_rel_bias, q_norm_g, k_norm_g,
           s5_lam_re, s5_lam_im, s5_log_dt, s5_b_re, s5_b_im, s5_c_re, s5_c_im, s5_d, s5_w_glu,
           w_br_a, w_br_s, w_br_c, w_out, w_router, b_router, w_exp1, b_exp1, w_exp2, b_exp2,
           final_norm_g):
    cc = jnp.concatenate([c, c_ctx[None], jnp.zeros((8 - BATCH - 1, D_MODEL), F32)], axis=0)
    mod = _ada_mod(cc, w_ada, b_ada)
    cos, sin = _rope_tables()
    w1b = w_exp1.astype(BF16)
    w2b = _w2_prep(w_exp2)
    b1r = b_exp1.reshape(DEPTH, N_EXPERTS, 1, 2 * D_FF_EXPERT)
    b2r = b_exp2.reshape(DEPTH, N_EXPERTS, 1, D_MODEL)
    xa = jnp.concatenate([x.reshape(N_LAT, D_MODEL), ctx.reshape(N_CTX, D_MODEL)], axis=0)
    for l in range(DEPTH):
        ctx_out = l < DEPTH - 1
        mod3 = mod[l].reshape(8, 1, 6 * D_MODEL)
        h = _norm_mod(xa, norm1_g[l], mod3, 0, 1)
        z = _matmul(h, w_in, l, F32)
        pt = _na_bias_table(na_rel_bias[l])
        ya = _na_attention(z, pt)
        qk = _qk_prep(z, q_norm_g[l], k_norm_g[l], cos, sin)
        yc = _gqa_attention(qk, z)
        s5w = _s5_weights(s5_lam_re[l], s5_lam_im[l], s5_log_dt[l], s5_b_re[l], s5_b_im[l],
                          s5_c_re[l], s5_c_im[l], s5_d[l])
        ys = _s5_branch(z, s5w, s5_w_glu, l)
        if ctx_out:
            ya_c = _ctx_attention(z, COL_QA // HEAD_DIM, z, COL_KA // HEAD_DIM, z, COL_VA // HEAD_DIM,
                                  NA_HEADS, 1)
            yc_c = _ctx_attention(qk, 0, qk, GQA_Q_WIDTH // HEAD_DIM, z, COL_VG // HEAD_DIM,
                                  GQA_KV_HEADS, GQA_GROUP)
            ya = jnp.concatenate([ya, ya_c], axis=0)
            yc = jnp.concatenate([yc, yc_c], axis=0)
        m = _gated_merge(ya, ys, yc, z, w_br_a, w_br_s, w_br_c, l)
        xa = _proj_residual(m, w_out, l, xa, mod3, 2)
        xa = _moe_ffn(xa, norm2_g[l], mod3, w_router[l], b_router[l], w1b, b1r, w2b, b2r, l)
    return _final_norm(xa[:N_LAT], final_norm_g).reshape(BATCH, SEQ, D_MODEL)
```

```python
import functools
import math

import jax
import jax.numpy as jnp
from jax import lax
from jax.experimental import pallas as pl
from jax.experimental.pallas import tpu as pltpu

F32 = jnp.float32
BF16 = jnp.bfloat16
I32 = jnp.int32

D_MODEL = 2048
BATCH = 4
SEQ = 2048
DEPTH = 2
GRID_W = 64
GRID_H = SEQ // GRID_W
CTX_LEN = 256
HEAD_DIM = 128
NA_HEADS = 8
NA_WIDTH = NA_HEADS * HEAD_DIM
NA_KH = 8
NA_KW = 16
GQA_HEADS = 8
GQA_KV_HEADS = 2
GQA_GROUP = GQA_HEADS // GQA_KV_HEADS
GQA_Q_WIDTH = GQA_HEADS * HEAD_DIM
GQA_KV_WIDTH = GQA_KV_HEADS * HEAD_DIM
ROPE_THETA = 10000.0
S5_WIDTH = 1024
S5_GROUP_CH = 16
S5_GROUPS = S5_WIDTH // S5_GROUP_CH
S5_STATE = 64
N_EXPERTS = 32
TOP_K = 4
D_FF_EXPERT = 1024
SWIGLU_ALPHA = 1.702
SWIGLU_LIMIT = 7.0
RMS_EPS = 1e-6

N_LAT = BATCH * SEQ
N_CTX = BATCH * CTX_LEN
N_ALL = N_LAT + N_CTX
CTX_WIDTH = 2 * NA_WIDTH + 2 * GQA_KV_WIDTH + S5_WIDTH
IN_WIDTH = CTX_WIDTH + NA_WIDTH + GQA_Q_WIDTH + 3 * D_MODEL
COL_KA = 0
COL_VA = NA_WIDTH
COL_KG = 2 * NA_WIDTH
COL_VG = COL_KG + GQA_KV_WIDTH
COL_U = COL_VG + GQA_KV_WIDTH
COL_QA = CTX_WIDTH
COL_QG = COL_QA + NA_WIDTH
COL_GA = COL_QG + GQA_Q_WIDTH
COL_GS = COL_GA + D_MODEL
COL_GC = COL_GS + D_MODEL

LANES = 128
ROW_CHUNKS = D_MODEL // LANES
NEG = -1e30
ATT_SCALE = HEAD_DIM ** -0.5

S5_CHUNK = 16
S5_SEQ = CTX_LEN + SEQ
S5_NCHUNK = S5_SEQ // S5_CHUNK
S5_CTX_CHUNKS = CTX_LEN // S5_CHUNK
S5_ROWS = S5_NCHUNK * BATCH
S5_UNITS = S5_GROUPS // 2
S5_LANES = S5_GROUPS * S5_STATE

MOE_BLOCK = 256
VMEM_LIMIT = 56 * 1024 * 1024


def _cparams(*sem):
    return pltpu.CompilerParams(dimension_semantics=sem, vmem_limit_bytes=VMEM_LIMIT)


def _sigmoid(x):
    return 1.0 / (1.0 + jnp.exp(-x))


def _batch_of_block(i, rows_per_block):
    return jnp.minimum(i // (SEQ // rows_per_block), BATCH)


def _ada_kernel(c_ref, w_ref, b_ref, o_ref):
    c = c_ref[...]
    a = (c * _sigmoid(c)).astype(BF16)
    o_ref[...] = jnp.dot(a, w_ref[...].astype(BF16), preferred_element_type=F32) + b_ref[...]


def _ada_mod(cc, w_ada, b_ada):
    tn = 1024
    return pl.pallas_call(
        _ada_kernel,
        grid=(DEPTH, 6 * D_MODEL // tn),
        in_specs=[
            pl.BlockSpec((8, D_MODEL), lambda l, j: (0, 0)),
            pl.BlockSpec((None, D_MODEL, tn), lambda l, j: (l, 0, j)),
            pl.BlockSpec((None, 1, tn), lambda l, j: (l, 0, j)),
        ],
        out_specs=pl.BlockSpec((None, 8, tn), lambda l, j: (l, 0, j)),
        out_shape=jax.ShapeDtypeStruct((DEPTH, 8, 6 * D_MODEL), F32),
        compiler_params=_cparams("parallel", "parallel"),
    )(cc, w_ada, b_ada.reshape(DEPTH, 1, 6 * D_MODEL))


def _normmod_kernel(x_ref, g_ref, sc_ref, sh_ref, o_ref):
    x = x_ref[...]
    y = x * lax.rsqrt(jnp.mean(x * x, axis=-1, keepdims=True) + RMS_EPS) * g_ref[...]
    o_ref[...] = (y * (1.0 + sc_ref[...]) + sh_ref[...]).astype(o_ref.dtype)


def _norm_mod(x, g, mod3, shift_idx, scale_idx):
    rows = x.shape[0]
    tm = 256
    return pl.pallas_call(
        _normmod_kernel,
        grid=(rows // tm,),
        in_specs=[
            pl.BlockSpec((tm, D_MODEL), lambda i: (i, 0)),
            pl.BlockSpec((1, D_MODEL), lambda i: (0, 0)),
            pl.BlockSpec((None, 1, D_MODEL), lambda i: (_batch_of_block(i, tm), 0, scale_idx)),
            pl.BlockSpec((None, 1, D_MODEL), lambda i: (_batch_of_block(i, tm), 0, shift_idx)),
        ],
        out_specs=pl.BlockSpec((tm, D_MODEL), lambda i: (i, 0)),
        out_shape=jax.ShapeDtypeStruct((rows, D_MODEL), BF16),
        compiler_params=_cparams("parallel"),
    )(x, g.reshape(1, D_MODEL), mod3, mod3)


def _final_norm_kernel(x_ref, g_ref, o_ref):
    x = x_ref[...]
    o_ref[...] = x * lax.rsqrt(jnp.mean(x * x, axis=-1, keepdims=True) + RMS_EPS) * g_ref[...]


def _final_norm(x, g):
    rows = x.shape[0]
    tm = 256
    return pl.pallas_call(
        _final_norm_kernel,
        grid=(rows // tm,),
        in_specs=[pl.BlockSpec((tm, D_MODEL), lambda i: (i, 0)),
                  pl.BlockSpec((1, D_MODEL), lambda i: (0, 0))],
        out_specs=pl.BlockSpec((tm, D_MODEL), lambda i: (i, 0)),
        out_shape=jax.ShapeDtypeStruct((rows, D_MODEL), F32),
        compiler_params=_cparams("parallel"),
    )(x, g.reshape(1, D_MODEL))


def _mm_kernel(a_ref, w_ref, o_ref):
    o_ref[...] = jnp.dot(a_ref[...], w_ref[...].astype(BF16),
                         preferred_element_type=F32).astype(o_ref.dtype)


def _matmul(a, w, l, out_dtype, tm=1024, tn=512):
    m, k = a.shape
    n = w.shape[2]
    return pl.pallas_call(
        _mm_kernel,
        grid=(m // tm, n // tn),
        in_specs=[pl.BlockSpec((tm, k), lambda i, j: (i, 0)),
                  pl.BlockSpec((None, k, tn), lambda i, j: (l, 0, j))],
        out_specs=pl.BlockSpec((tm, tn), lambda i, j: (i, j)),
        out_shape=jax.ShapeDtypeStruct((m, n), out_dtype),
        compiler_params=_cparams("parallel", "parallel"),
    )(a, w)


NA_QROWS = 4
NA_BAND = NA_KH + NA_QROWS
NA_TQ = NA_QROWS * GRID_W
NA_TK = NA_BAND * GRID_W


def _na_bias_table(rel_bias):
    cidx = jnp.arange(GRID_W, dtype=I32)
    col_start = jnp.clip(cidx - NA_KW // 2, 0, GRID_W - NA_KW)
    col_ok = (cidx[None, :] >= col_start[:, None]) & (cidx[None, :] < col_start[:, None] + NA_KW)
    dc = jnp.clip(cidx[None, :] - cidx[:, None] + (NA_KW - 1), 0, 2 * NA_KW - 2)
    t = rel_bias.astype(F32)[:, :, dc]
    t = jnp.where(col_ok[None, None], t, NEG)
    pad = jnp.full((NA_HEADS, 1, GRID_W, GRID_W), NEG, F32)
    t = jnp.concatenate([pad, t, pad], axis=1)
    return jnp.concatenate([t[:, :-1], t[:, 1:]], axis=-1)


def _na_kernel(q_ref, k_ref, v_ref, kc_ref, vc_ref, pt_ref, o_ref):
    i = pl.program_id(2)
    start = jnp.clip(NA_QROWS * i - NA_KH // 2, 0, GRID_H - NA_BAND)
    koff = pl.multiple_of(start * GRID_W, GRID_W * NA_QROWS)
    q = q_ref[...].astype(BF16)
    kb = k_ref[pl.ds(koff, NA_TK), :].astype(BF16)
    vb = v_ref[pl.ds(koff, NA_TK), :].astype(BF16)
    nt = (((1,), (1,)), ((), ()))
    s = lax.dot_general(q, kb, nt, preferred_element_type=F32) * ATT_SCALE
    lane = lax.broadcasted_iota(I32, (GRID_W, 2 * GRID_W), 1)
    bias_rows = []
    for a in range(NA_QROWS):
        rq = NA_QROWS * i + a
        ws = jnp.clip(rq - NA_KH // 2, 0, GRID_H - NA_KH)
        tiles = []
        for jp in range(NA_BAND // 2):
            rk = start + 2 * jp
            e = jnp.clip(rk - rq + NA_KH, 0, 2 * NA_KH - 1)
            ok0 = ((rk >= ws) & (rk < ws + NA_KH)).astype(I32)
            ok1 = ((rk + 1 >= ws) & (rk + 1 < ws + NA_KH)).astype(I32)
            ok = jnp.where(lane < GRID_W, ok0, ok1) > 0
            tiles.append(jnp.where(ok, pt_ref[e], NEG))
        bias_rows.append(jnp.concatenate(tiles, axis=1))
    s = s + jnp.concatenate(bias_rows, axis=0)
    sc = lax.dot_general(q, kc_ref[...].astype(BF16), nt, preferred_element_type=F32) * ATT_SCALE
    m = jnp.maximum(jnp.max(s, axis=-1, keepdims=True), jnp.max(sc, axis=-1, keepdims=True))
    p = jnp.exp(s - m)
    pc = jnp.exp(sc - m)
    den = jnp.sum(p, axis=-1, keepdims=True) + jnp.sum(pc, axis=-1, keepdims=True)
    o = (jnp.dot(p.astype(BF16), vb, preferred_element_type=F32)
         + jnp.dot(pc.astype(BF16), vc_ref[...].astype(BF16), preferred_element_type=F32))
    o_ref[...] = (o / den).astype(o_ref.dtype)


def _na_attention(z, pt):
    qb = SEQ // NA_TQ
    cq, ck, cv = COL_QA // HEAD_DIM, COL_KA // HEAD_DIM, COL_VA // HEAD_DIM
    ctx_blk = N_LAT // CTX_LEN
    return pl.pallas_call(
        _na_kernel,
        grid=(BATCH, NA_HEADS, qb),
        in_specs=[
            pl.BlockSpec((NA_TQ, HEAD_DIM), lambda b, h, i: (b * qb + i, cq + h)),
            pl.BlockSpec((SEQ, HEAD_DIM), lambda b, h, i: (b, ck + h)),
            pl.BlockSpec((SEQ, HEAD_DIM), lambda b, h, i: (b, cv + h)),
            pl.BlockSpec((CTX_LEN, HEAD_DIM), lambda b, h, i: (ctx_blk + b, ck + h)),
            pl.BlockSpec((CTX_LEN, HEAD_DIM), lambda b, h, i: (ctx_blk + b, cv + h)),
            pl.BlockSpec((None, 2 * NA_KH, GRID_W, 2 * GRID_W), lambda b, h, i: (h, 0, 0, 0)),
        ],
        out_specs=pl.BlockSpec((NA_TQ, HEAD_DIM), lambda b, h, i: (b * qb + i, h)),
        out_shape=jax.ShapeDtypeStruct((N_LAT, NA_WIDTH), BF16),
        compiler_params=_cparams("parallel", "parallel", "parallel"),
    )(z, z, z, z, z, pt)


def _softmax_attend(q, ks, vs):
    nt = (((1,), (1,)), ((), ()))
    ss = [lax.dot_general(q, k, nt, preferred_element_type=F32) * ATT_SCALE for k in ks]
    m = ss[0].max(axis=-1, keepdims=True)
    for s in ss[1:]:
        m = jnp.maximum(m, s.max(axis=-1, keepdims=True))
    ps = [jnp.exp(s - m) for s in ss]
    den = ps[0].sum(axis=-1, keepdims=True)
    for p in ps[1:]:
        den = den + p.sum(axis=-1, keepdims=True)
    o = jnp.dot(ps[0].astype(BF16), vs[0], preferred_element_type=F32)
    for p, v in zip(ps[1:], vs[1:]):
        o = o + jnp.dot(p.astype(BF16), v, preferred_element_type=F32)
    return o / den


def _stack_heads(q, group):
    return jnp.concatenate([q[:, g * HEAD_DIM:(g + 1) * HEAD_DIM] for g in range(group)], axis=0)


def _unstack_heads(o, group, rows):
    return jnp.concatenate([o[g * rows:(g + 1) * rows] for g in range(group)], axis=1)


def _ctx_attn_kernel(q_ref, k_ref, v_ref, o_ref, *, group):
    q = _stack_heads(q_ref[...].astype(BF16), group)
    o = _softmax_attend(q, [k_ref[...].astype(BF16)], [v_ref[...].astype(BF16)])
    o_ref[...] = _unstack_heads(o, group, CTX_LEN).astype(o_ref.dtype)


def _ctx_attention(q_arr, q_col, k_arr, k_col, v_arr, v_col, kv_heads, group):
    ctx_blk = N_LAT // CTX_LEN
    qw = group * HEAD_DIM
    return pl.pallas_call(
        functools.partial(_ctx_attn_kernel, group=group),
        grid=(BATCH, kv_heads),
        in_specs=[
            pl.BlockSpec((CTX_LEN, qw), lambda b, h: (ctx_blk + b, q_col // group + h)),
            pl.BlockSpec((CTX_LEN, HEAD_DIM), lambda b, h: (ctx_blk + b, k_col + h)),
            pl.BlockSpec((CTX_LEN, HEAD_DIM), lambda b, h: (ctx_blk + b, v_col + h)),
        ],
        out_specs=pl.BlockSpec((CTX_LEN, qw), lambda b, h: (b, h)),
        out_shape=jax.ShapeDtypeStruct((N_CTX, kv_heads * qw), BF16),
        compiler_params=_cparams("parallel", "parallel"),
    )(q_arr, k_arr, v_arr)


def _rope_tables():
    t = jnp.arange(SEQ, dtype=I32)
    row = (t // GRID_W).astype(F32)
    col = (t % GRID_W).astype(F32)
    n_freq = HEAD_DIM // 4
    inv_freq = ROPE_THETA ** (-jnp.arange(n_freq, dtype=F32) / n_freq)
    ar = row[:, None] * inv_freq
    ac = col[:, None] * inv_freq
    cos = jnp.concatenate([jnp.cos(ar), jnp.cos(ar), jnp.cos(ac), jnp.cos(ac)], axis=-1)
    sin = jnp.concatenate([-jnp.sin(ar), jnp.sin(ar), -jnp.sin(ac), jnp.sin(ac)], axis=-1)
    cos = jnp.concatenate([cos, jnp.ones((CTX_LEN, HEAD_DIM), F32)], axis=0)
    sin = jnp.concatenate([sin, jnp.zeros((CTX_LEN, HEAD_DIM), F32)], axis=0)
    return cos, sin


def _qkprep_kernel(q0_ref, q1_ref, k_ref, cos_ref, sin_ref, qg_ref, kg_ref, o_ref):
    c = cos_ref[...]
    s = sin_ref[...]
    lane = lax.broadcasted_iota(I32, c.shape, 1)
    first = (lane % (HEAD_DIM // 2)) < (HEAD_DIM // 4)

    def prep(x, g):
        y = x * lax.rsqrt(jnp.mean(x * x, axis=-1, keepdims=True) + RMS_EPS) * g
        partner = jnp.where(first, pltpu.roll(y, HEAD_DIM - HEAD_DIM // 4, 1),
                            pltpu.roll(y, HEAD_DIM // 4, 1))
        return (y * c + partner * s).astype(o_ref.dtype)

    half = GQA_HEADS // 2
    for h in range(half):
        sl = slice(h * HEAD_DIM, (h + 1) * HEAD_DIM)
        o_ref[:, sl] = prep(q0_ref[:, sl], qg_ref[...])
        o_ref[:, (half + h) * HEAD_DIM:(half + h + 1) * HEAD_DIM] = prep(q1_ref[:, sl], qg_ref[...])
    for h in range(GQA_KV_HEADS):
        sl = slice(h * HEAD_DIM, (h + 1) * HEAD_DIM)
        o_ref[:, (GQA_HEADS + h) * HEAD_DIM:(GQA_HEADS + h + 1) * HEAD_DIM] = prep(k_ref[:, sl], kg_ref[...])


def _qk_prep(z, q_norm_g, k_norm_g, cos, sin):
    tm = 256
    hq = GQA_Q_WIDTH // 2
    lat_blocks = N_LAT // tm
    per_batch = SEQ // tm
    tbl = lambda i: (jnp.where(i < lat_blocks, i % per_batch, per_batch), 0)
    return pl.pallas_call(
        _qkprep_kernel,
        grid=(N_ALL // tm,),
        in_specs=[
            pl.BlockSpec((tm, hq), lambda i: (i, COL_QG // hq)),
            pl.BlockSpec((tm, hq), lambda i: (i, COL_QG // hq + 1)),
            pl.BlockSpec((tm, GQA_KV_WIDTH), lambda i: (i, COL_KG // GQA_KV_WIDTH)),
            pl.BlockSpec((tm, HEAD_DIM), tbl),
            pl.BlockSpec((tm, HEAD_DIM), tbl),
            pl.BlockSpec((1, HEAD_DIM), lambda i: (0, 0)),
            pl.BlockSpec((1, HEAD_DIM), lambda i: (0, 0)),
        ],
        out_specs=pl.BlockSpec((tm, GQA_Q_WIDTH + GQA_KV_WIDTH), lambda i: (i, 0)),
        out_shape=jax.ShapeDtypeStruct((N_ALL, GQA_Q_WIDTH + GQA_KV_WIDTH), BF16),
        compiler_params=_cparams("parallel"),
    )(z, z, z, cos, sin, q_norm_g.reshape(1, HEAD_DIM), k_norm_g.reshape(1, HEAD_DIM))


GQA_TQ = 256


def _gqa_kernel(q_ref, k_ref, kc_ref, v_ref, vc_ref, o_ref):
    q = _stack_heads(q_ref[...], GQA_GROUP)
    o = _softmax_attend(q, [k_ref[...], kc_ref[...]],
                        [v_ref[...].astype(BF16), vc_ref[...].astype(BF16)])
    o_ref[...] = _unstack_heads(o, GQA_GROUP, GQA_TQ).astype(o_ref.dtype)


def _gqa_attention(qk, z):
    qb = SEQ // GQA_TQ
    qw = GQA_GROUP * HEAD_DIM
    kcol = GQA_Q_WIDTH // HEAD_DIM
    vcol = COL_VG // HEAD_DIM
    ctx_blk = N_LAT // CTX_LEN
    return pl.pallas_call(
        _gqa_kernel,
        grid=(BATCH, GQA_KV_HEADS, qb),
        in_specs=[
            pl.BlockSpec((GQA_TQ, qw), lambda b, h, i: (b * qb + i, h)),
            pl.BlockSpec((SEQ, HEAD_DIM), lambda b, h, i: (b, kcol + h)),
            pl.BlockSpec((CTX_LEN, HEAD_DIM), lambda b, h, i: (ctx_blk + b, kcol + h)),
            pl.BlockSpec((SEQ, HEAD_DIM), lambda b, h, i: (b, vcol + h)),
            pl.BlockSpec((CTX_LEN, HEAD_DIM), lambda b, h, i: (ctx_blk + b, vcol + h)),
        ],
        out_specs=pl.BlockSpec((GQA_TQ, qw), lambda b, h, i: (b * qb + i, h)),
        out_shape=jax.ShapeDtypeStruct((N_LAT, GQA_Q_WIDTH), BF16),
        compiler_params=_cparams("parallel", "parallel", "parallel"),
    )(qk, qk, qk, z, z)


def _pair_blockdiag(x):
    g, a, b = x.shape
    x2 = x.reshape(g // 2, 2, a, b)
    eye = jnp.eye(2, dtype=x.dtype)
    return jnp.einsum('qiab,ij->qiajb', x2, eye).reshape(g // 2, 2 * a, 2 * b)


def _s5_weights(lam_re, lam_im, log_dt, b_re, b_im, c_re, c_im, d_skip):
    hi = lax.Precision.HIGHEST
    G, N, P, C = S5_GROUPS, S5_STATE, S5_GROUP_CH, S5_CHUNK
    lam_re, lam_im = lam_re.astype(F32), lam_im.astype(F32)
    dt = jnp.exp(log_dt.astype(F32))[..., None]
    tau = jnp.arange(C + 1, dtype=F32)[None, None, :, None]
    mag = jnp.exp(lam_re[:, :, None, :] * dt[:, :, None, :] * tau)
    ang = lam_im[:, :, None, :] * dt[:, :, None, :] * tau
    lp_re, lp_im = mag * jnp.cos(ang), mag * jnp.sin(ang)
    nr, ni = lp_re[:, :, 1] - 1.0, lp_im[:, :, 1]
    den = lam_re * lam_re + lam_im * lam_im
    fr = (nr * lam_re + ni * lam_im) / den
    fi = (ni * lam_re - nr * lam_im) / den
    bb_re = fr[..., None] * b_re - fi[..., None] * b_im
    bb_im = fr[..., None] * b_im + fi[..., None] * b_re
    c_re, c_im = c_re.astype(F32), c_im.astype(F32)
    lb_re = lp_re[..., None] * bb_re[:, :, None] - lp_im[..., None] * bb_im[:, :, None]
    lb_im = lp_re[..., None] * bb_im[:, :, None] + lp_im[..., None] * bb_re[:, :, None]
    kern = (jnp.einsum('dgon,dgtni->dgtoi', c_re, lb_re[:, :, :C], precision=hi)
            - jnp.einsum('dgon,dgtni->dgtoi', c_im, lb_im[:, :, :C], precision=hi))
    eye_p = jnp.eye(P, dtype=F32)
    kern = kern.at[0, :, 0].add(eye_p[None] * d_skip.astype(F32).reshape(G, P, 1))
    k2 = kern.transpose(0, 1, 4, 2, 3).reshape(2, G * P, C * P)
    tau_i = jnp.arange(C)[:, None, None]
    s_i = jnp.arange(C)[None, :, None]
    t_i = jnp.arange(C)[None, None, :]
    shifts = jnp.stack([t_i - s_i == tau_i, s_i - t_i == tau_i]).astype(F32)
    sel = jnp.einsum('dzst,po->dzpsto', shifts, eye_p).reshape(2 * C * P, C * C * P)
    m = jnp.dot(jnp.concatenate([k2[0], k2[1]], axis=1), sel, precision=hi)
    m = m.reshape(G, P, C, C * P).transpose(0, 2, 1, 3).reshape(G, C * P, C * P)
    wsf_re = lb_re[0][:, C - 1::-1][:, :C].transpose(0, 1, 3, 2).reshape(G, C * P, N)
    wsf_im = lb_im[0][:, C - 1::-1][:, :C].transpose(0, 1, 3, 2).reshape(G, C * P, N)
    wsr_re = lb_re[1][:, :C].transpose(0, 1, 3, 2).reshape(G, C * P, N)
    wsr_im = lb_im[1][:, :C].transpose(0, 1, 3, 2).reshape(G, C * P, N)
    def state_out(d, powers_re, powers_im):
        cr, ci = c_re[d][:, None], c_im[d][:, None]
        pr, pi_ = powers_re[:, :, None], powers_im[:, :, None]
        from_re = (cr * pr - ci * pi_).transpose(0, 3, 1, 2).reshape(G, N, C * P)
        from_im = (-(cr * pi_ + ci * pr)).transpose(0, 3, 1, 2).reshape(G, N, C * P)
        return from_re, from_im
    of_re, of_im = state_out(0, lp_re[0][:, 1:C + 1], lp_im[0][:, 1:C + 1])
    or_re, or_im = state_out(1, lp_re[1][:, C:0:-1], lp_im[1][:, C:0:-1])
    w_state = jnp.concatenate([_pair_blockdiag(w) for w in (wsf_re, wsf_im, wsr_re, wsr_im)], axis=-1)
    w_out = jnp.concatenate([_pair_blockdiag(w) for w in (m, of_re, of_im, or_re, or_im)], axis=1)
    lam_c = jnp.stack([lp_re[0][:, C], lp_im[0][:, C], lp_re[1][:, C], lp_im[1][:, C]])
    return w_state.astype(BF16), w_out.astype(BF16), lam_c.reshape(4, 1, S5_LANES)


def _s5_state_kernel(u_ref, w_ref, fre_ref, fim_ref, rre_ref, rim_ref):
    r = jnp.dot(u_ref[...], w_ref[...], preferred_element_type=F32)
    for k, ref in enumerate((fre_ref, fim_ref, rre_ref, rim_ref)):
        ref[...] = r[:, k * LANES:(k + 1) * LANES]


def _s5_local_states(u2, w_state, l):
    uw = 2 * S5_CHUNK * S5_GROUP_CH
    out = jax.ShapeDtypeStruct((S5_ROWS, S5_LANES), F32)
    ospec = pl.BlockSpec((S5_ROWS, LANES), lambda q: (0, q))
    return pl.pallas_call(
        _s5_state_kernel,
        grid=(S5_UNITS,),
        in_specs=[pl.BlockSpec((None, S5_ROWS, uw), lambda q: (q, 0, 0)),
                  pl.BlockSpec((None, None, uw, 4 * LANES), lambda q: (l, q, 0, 0))],
        out_specs=[ospec] * 4,
        out_shape=[out] * 4,
        compiler_params=_cparams("parallel"),
    )(u2, w_state)


S5_SCAN_LANES = 512
S5_PAIR_ROWS = 2 * BATCH


def _s5_scan_kernel(fre_ref, fim_ref, rre_ref, rim_ref, lam_ref,
                    pfre_ref, pfim_ref, prre_ref, prim_ref):
    lam = lam_ref[...]
    shape = (S5_PAIR_ROWS, S5_SCAN_LANES)
    top = lax.broadcasted_iota(I32, shape, 0) < BATCH
    n_pairs = S5_NCHUNK // 2
    ctx_pairs = S5_CTX_CHUNKS // 2

    def cmul_add(ar, ai, sr, si, lr, li):
        return ar * sr - ai * si + lr, ar * si + ai * sr + li

    def half_step(ar, ai, sr, si, lr, li, first_top):
        xr, xi = cmul_add(ar, ai, sr, si, lr, li)
        xr, xi = pltpu.roll(xr, BATCH, 0), pltpu.roll(xi, BATCH, 0)
        keep = top if first_top else ~top
        prev_r, prev_i = jnp.where(keep, sr, xr), jnp.where(keep, si, xi)
        yr, yi = cmul_add(ar, ai, xr, xi, lr, li)
        nr = jnp.where(keep, pltpu.roll(yr, BATCH, 0), yr)
        ni = jnp.where(keep, pltpu.roll(yi, BATCH, 0), yi)
        return prev_r, prev_i, nr, ni

    def body(kk, carry):
        sfr, sfi, srr, sri = carry
        fo = pl.multiple_of(kk * S5_PAIR_ROWS, S5_PAIR_ROWS)
        rp = jnp.where(kk < ctx_pairs, ctx_pairs - 1 - kk, n_pairs - 1 + ctx_pairs - kk)
        ro = pl.multiple_of(rp * S5_PAIR_ROWS, S5_PAIR_ROWS)
        pr, pi_, sfr, sfi = half_step(lam[0], lam[1], sfr, sfi,
                                      fre_ref[pl.ds(fo, S5_PAIR_ROWS), :],
                                      fim_ref[pl.ds(fo, S5_PAIR_ROWS), :], True)
        pfre_ref[pl.ds(fo, S5_PAIR_ROWS), :] = pr
        pfim_ref[pl.ds(fo, S5_PAIR_ROWS), :] = pi_
        pr, pi_, srr, sri = half_step(lam[2], lam[3], srr, sri,
                                      rre_ref[pl.ds(ro, S5_PAIR_ROWS), :],
                                      rim_ref[pl.ds(ro, S5_PAIR_ROWS), :], False)
        prre_ref[pl.ds(ro, S5_PAIR_ROWS), :] = pr
        prim_ref[pl.ds(ro, S5_PAIR_ROWS), :] = pi_
        return sfr, sfi, srr, sri

    zero = jnp.zeros(shape, F32)
    lax.fori_loop(0, n_pairs, body, (zero, zero, zero, zero))


def _s5_chunk_scan(loc, lam_c, l):
    spec = pl.BlockSpec((S5_ROWS, S5_SCAN_LANES), lambda j: (0, j))
    out = jax.ShapeDtypeStruct((S5_ROWS, S5_LANES), F32)
    return pl.pallas_call(
        _s5_scan_kernel,
        grid=(S5_LANES // S5_SCAN_LANES,),
        in_specs=[spec] * 4 + [pl.BlockSpec((None, 4, 1, S5_SCAN_LANES), lambda j: (l, 0, 0, j))],
        out_specs=[spec] * 4,
        out_shape=[out] * 4,
        compiler_params=_cparams("parallel"),
    )(*loc, lam_c)


def _s5_out_kernel(u_ref, fre_ref, fim_ref, rre_ref, rim_ref, w_ref, o_ref):
    lhs = jnp.concatenate([u_ref[...]] + [r[...].astype(BF16)
                                           for r in (fre_ref, fim_ref, rre_ref, rim_ref)], axis=1)
    o_ref[...] = jnp.dot(lhs, w_ref[...], preferred_element_type=F32)


def _s5_outputs(u2, prev, w_out, l):
    uw = 2 * S5_CHUNK * S5_GROUP_CH
    sspec = pl.BlockSpec((S5_ROWS, LANES), lambda q: (0, q))
    return pl.pallas_call(
        _s5_out_kernel,
        grid=(S5_UNITS,),
        in_specs=[pl.BlockSpec((None, S5_ROWS, uw), lambda q: (q, 0, 0))] + [sspec] * 4
                 + [pl.BlockSpec((None, None, uw + 4 * LANES, uw), lambda q: (l, q, 0, 0))],
        out_specs=pl.BlockSpec((None, S5_ROWS, uw), lambda q: (q, 0, 0)),
        out_shape=jax.ShapeDtypeStruct((S5_UNITS, S5_ROWS, uw), F32),
        compiler_params=_cparams("parallel"),
    )(u2, *prev, w_out)


def _glu_kernel(y_ref, w_ref, o_ref):
    y = y_ref[...]
    g = 0.5 * y * (1.0 + jnp.tanh(math.sqrt(2.0 / math.pi) * (y + 0.044715 * (y * y * y))))
    t = jnp.dot(g.astype(BF16), w_ref[...].astype(BF16), preferred_element_type=F32)
    o_ref[...] = (g * _sigmoid(t)).astype(o_ref.dtype)


def _s5_glu(y, w_glu, l):
    rows = y.shape[0]
    tm = 512
    return pl.pallas_call(
        _glu_kernel,
        grid=(rows // tm,),
        in_specs=[pl.BlockSpec((tm, S5_WIDTH), lambda i: (i, 0)),
                  pl.BlockSpec((None, S5_WIDTH, S5_WIDTH), lambda i: (l, 0, 0))],
        out_specs=pl.BlockSpec((tm, S5_WIDTH), lambda i: (i, 0)),
        out_shape=jax.ShapeDtypeStruct((rows, S5_WIDTH), BF16),
        compiler_params=_cparams("parallel"),
    )(y, w_glu)


def _s5_branch(z, s5w, w_glu, l):
    w_state, w_out, lam_c = s5w
    C, P = S5_CHUNK, S5_GROUP_CH
    u = lax.optimization_barrier(z[:, COL_U:COL_U + S5_WIDTH])
    u_seq = jnp.concatenate([u[N_LAT:].reshape(BATCH, CTX_LEN, S5_WIDTH),
                             u[:N_LAT].reshape(BATCH, SEQ, S5_WIDTH)], axis=1)
    u2 = (u_seq.astype(BF16).reshape(BATCH, S5_NCHUNK, C, S5_UNITS, 2, P)
          .transpose(3, 1, 0, 4, 2, 5).reshape(S5_UNITS, S5_ROWS, 2 * C * P))
    loc = _s5_local_states(u2, w_state, l)
    prev = _s5_chunk_scan(loc, lam_c, l)
    y2 = _s5_outputs(u2, prev, w_out, l)
    y = (y2.reshape(S5_UNITS, S5_NCHUNK, BATCH, 2, C, P)
         .transpose(2, 1, 4, 0, 3, 5).reshape(BATCH, S5_SEQ, S5_WIDTH))
    y_all = jnp.concatenate([y[:, CTX_LEN:].reshape(N_LAT, S5_WIDTH),
                             y[:, :CTX_LEN].reshape(N_CTX, S5_WIDTH)], axis=0)
    return _s5_glu(y_all, w_glu, l)


def _merge_kernel(ya_ref, ys_ref, yc_ref, wa_ref, ws_ref, wc_ref, ga_ref, gs_ref, gc_ref, o_ref):
    def branch(y_ref, w_ref, g_ref):
        return _sigmoid(g_ref[...]) * jnp.dot(y_ref[...], w_ref[...].astype(BF16),
                                              preferred_element_type=F32)
    o_ref[...] = (branch(ya_ref, wa_ref, ga_ref) + branch(ys_ref, ws_ref, gs_ref)
                  + branch(yc_ref, wc_ref, gc_ref)).astype(o_ref.dtype)


def _gated_merge(ya, ys, yc, z, w_br_a, w_br_s, w_br_c, l):
    rows = ya.shape[0]
    tm, tn = 1024, 512
    aspec = pl.BlockSpec((tm, NA_WIDTH), lambda i, j: (i, 0))
    wspec = pl.BlockSpec((None, NA_WIDTH, tn), lambda i, j: (l, 0, j))
    gspec = lambda col: pl.BlockSpec((tm, tn), lambda i, j: (i, col // tn + j))
    return pl.pallas_call(
        _merge_kernel,
        grid=(rows // tm, D_MODEL // tn),
        in_specs=[aspec] * 3 + [wspec] * 3 + [gspec(COL_GA), gspec(COL_GS), gspec(COL_GC)],
        out_specs=pl.BlockSpec((tm, tn), lambda i, j: (i, j)),
        out_shape=jax.ShapeDtypeStruct((rows, D_MODEL), BF16),
        compiler_params=_cparams("parallel", "parallel"),
    )(ya, ys, yc, w_br_a, w_br_s, w_br_c, z, z, z)


def _proj_res_kernel(m_ref, w_ref, x_ref, g_ref, o_ref):
    y = jnp.dot(m_ref[...], w_ref[...].astype(BF16), preferred_element_type=F32)
    o_ref[...] = x_ref[...] + g_ref[...] * y


def _proj_residual(m, w_out, l, x, mod3, gate_idx):
    rows = m.shape[0]
    tm, tn = 1024, 512
    nj = D_MODEL // tn
    return pl.pallas_call(
        _proj_res_kernel,
        grid=(rows // tm, nj),
        in_specs=[
            pl.BlockSpec((tm, D_MODEL), lambda i, j: (i, 0)),
            pl.BlockSpec((None, D_MODEL, tn), lambda i, j: (l, 0, j)),
            pl.BlockSpec((tm, tn), lambda i, j: (i, j)),
            pl.BlockSpec((None, 1, tn), lambda i, j: (_batch_of_block(i, tm), 0, gate_idx * nj + j)),
        ],
        out_specs=pl.BlockSpec((tm, tn), lambda i, j: (i, j)),
        out_shape=jax.ShapeDtypeStruct((rows, D_MODEL), F32),
        compiler_params=_cparams("parallel", "parallel"),
    )(m, w_out, x, mod3)


def _router_kernel(x_ref, g_ref, sc_ref, sh_ref, wr_ref, br_ref, h_ref, idx_ref, wt_ref, rank_ref,
                   cnt_ref, count):
    @pl.when(pl.program_id(0) == 0)
    def _():
        count[...] = jnp.zeros(count.shape, count.dtype)

    x = x_ref[...]
    tm = x.shape[0]
    y = x * lax.rsqrt(jnp.mean(x * x, axis=-1, keepdims=True) + RMS_EPS) * g_ref[...]
    h = y * (1.0 + sc_ref[...]) + sh_ref[...]
    for k in range(ROW_CHUNKS):
        h_ref[pl.ds(k, tm, stride=ROW_CHUNKS), :] = h[:, k * LANES:(k + 1) * LANES]
    logits = jnp.dot(h, wr_ref[...], preferred_element_type=F32,
                     precision=lax.Precision.HIGHEST) + br_ref[...]
    lane = lax.broadcasted_iota(I32, logits.shape, 1)
    vals, idxs = [], []
    for _ in range(TOP_K):
        m = jnp.max(logits, axis=-1, keepdims=True)
        sel = jnp.min(jnp.where(logits == m, lane, LANES), axis=-1, keepdims=True)
        vals.append(m)
        idxs.append(sel)
        logits = jnp.where(lane == sel, NEG, logits)
    es = [jnp.exp(v - vals[0]) for v in vals]
    den = es[0] + es[1] + es[2] + es[3]
    hits = [lane == idxs[k] for k in range(TOP_K)]
    per_expert = hits[0].astype(F32)
    for k in range(1, TOP_K):
        per_expert = per_expert + hits[k].astype(F32)
    tri = (lax.broadcasted_iota(I32, (tm, tm), 1) < lax.broadcasted_iota(I32, (tm, tm), 0)).astype(BF16)
    before = jnp.dot(tri, per_expert.astype(BF16), preferred_element_type=F32) + count[...]
    idx_out = jnp.zeros(lane.shape, I32)
    wt_out = jnp.zeros(lane.shape, F32)
    rank_out = jnp.zeros(lane.shape, I32)
    for k in range(TOP_K):
        rank_k = jnp.sum(jnp.where(hits[k], before, 0.0), axis=-1, keepdims=True).astype(I32)
        idx_out = jnp.where(lane == k, idxs[k], idx_out)
        wt_out = jnp.where(lane == k, es[k] / den, wt_out)
        rank_out = jnp.where(lane == k, rank_k, rank_out)
    idx_ref[...] = idx_out
    wt_ref[...] = wt_out
    rank_ref[...] = rank_out
    count[...] = count[...] + jnp.sum(per_expert, axis=0, keepdims=True)
    cnt_ref[...] = count[...]


def _norm_router(x, g, mod3, shift_idx, scale_idx, w_router, b_router):
    rows = x.shape[0]
    tm = 256
    wr = jnp.zeros((D_MODEL, LANES), F32).at[:, :N_EXPERTS].set(w_router)
    br = jnp.full((1, LANES), NEG, F32).at[0, :N_EXPERTS].set(b_router)
    small = pl.BlockSpec((tm, LANES), lambda i: (i, 0))
    return pl.pallas_call(
        _router_kernel,
        grid=(rows // tm,),
        in_specs=[
            pl.BlockSpec((tm, D_MODEL), lambda i: (i, 0)),
            pl.BlockSpec((1, D_MODEL), lambda i: (0, 0)),
            pl.BlockSpec((None, 1, D_MODEL), lambda i: (_batch_of_block(i, tm), 0, scale_idx)),
            pl.BlockSpec((None, 1, D_MODEL), lambda i: (_batch_of_block(i, tm), 0, shift_idx)),
            pl.BlockSpec((D_MODEL, LANES), lambda i: (0, 0)),
            pl.BlockSpec((1, LANES), lambda i: (0, 0)),
        ],
        out_specs=[pl.BlockSpec((tm * ROW_CHUNKS, LANES), lambda i: (i, 0)), small, small, small,
                   pl.BlockSpec((1, LANES), lambda i: (0, 0))],
        out_shape=[jax.ShapeDtypeStruct((rows * ROW_CHUNKS, LANES), F32),
                   jax.ShapeDtypeStruct((rows, LANES), I32),
                   jax.ShapeDtypeStruct((rows, LANES), F32),
                   jax.ShapeDtypeStruct((rows, LANES), I32),
                   jax.ShapeDtypeStruct((1, LANES), F32)],
        scratch_shapes=[pltpu.VMEM((1, LANES), F32)],
        compiler_params=_cparams("arbitrary"),
    )(x, g.reshape(1, D_MODEL), mod3, mod3, wr, br)


W2_PREP_ROWS = 256


def _w2_prep_kernel(w_ref, o_ref, scr):
    half = LANES // 2
    for k in range(ROW_CHUNKS):
        cols = slice(k * LANES, (k + 1) * LANES)
        s = scr.at[k]
        for t in range(W2_PREP_ROWS // LANES):
            s[pl.ds(t * LANES, half, stride=2), :] = w_ref[t * LANES:t * LANES + half, cols]
            s[pl.ds(t * LANES + 1, half, stride=2), :] = w_ref[t * LANES + half:(t + 1) * LANES, cols]
        o_ref[:, cols] = s[...].astype(o_ref.dtype)


def _w2_prep(w2):
    de = DEPTH * N_EXPERTS
    spec = pl.BlockSpec((None, W2_PREP_ROWS, D_MODEL), lambda e, r: (e, r, 0))
    out = pl.pallas_call(
        _w2_prep_kernel,
        grid=(de, D_FF_EXPERT // W2_PREP_ROWS),
        in_specs=[spec],
        out_specs=spec,
        out_shape=jax.ShapeDtypeStruct((de, D_FF_EXPERT, D_MODEL), BF16),
        scratch_shapes=[pltpu.VMEM((ROW_CHUNKS, W2_PREP_ROWS, LANES), F32)],
        compiler_params=_cparams("parallel", "parallel"),
    )(w2.reshape(de, D_FF_EXPERT, D_MODEL))
    return out.reshape(DEPTH, N_EXPERTS, D_FF_EXPERT, D_MODEL)


def _expert_kernel(be_ref, tok_ref, nused_ref, h_hbm, w1_ref, b1_ref, w2_ref, b2_ref,
                   o_ref, gbuf, xb, sem):
    i = pl.program_id(0)
    n_used = nused_ref[0]
    slot = i % 2

    def start_gather(blk, s):
        base = blk * MOE_BLOCK

        def issue(j, c):
            pltpu.make_async_copy(h_hbm.at[tok_ref[base + j]],
                                  gbuf.at[s, pl.ds(j * ROW_CHUNKS, ROW_CHUNKS), :], sem.at[s]).start()
            return c
        lax.fori_loop(0, MOE_BLOCK, issue, 0, unroll=8)

    def wait_gather(s):
        def wait(j, c):
            pltpu.make_async_copy(h_hbm.at[0], gbuf.at[s, pl.ds(j * ROW_CHUNKS, ROW_CHUNKS), :],
                                  sem.at[s]).wait()
            return c
        lax.fori_loop(0, MOE_BLOCK, wait, 0, unroll=8)

    @pl.when(i == 0)
    def _():
        start_gather(0, 0)

    @pl.when(i + 1 < n_used)
    def _():
        start_gather(i + 1, 1 - slot)

    @pl.when(i < n_used)
    def _():
        wait_gather(slot)
        g = gbuf.at[slot]
        for k in range(ROW_CHUNKS):
            xb[:, k * LANES:(k + 1) * LANES] = g[pl.ds(k, MOE_BLOCK, stride=ROW_CHUNKS), :].astype(BF16)
        a = jnp.dot(xb[...], w1_ref[...], preferred_element_type=F32) + b1_ref[...]
        even = lax.broadcasted_iota(I32, (MOE_BLOCK, LANES), 1) % 2 == 0
        prods = []
        for s in range(2 * D_FF_EXPERT // LANES):
            a_s = a[:, s * LANES:(s + 1) * LANES]
            gate = jnp.minimum(a_s, SWIGLU_LIMIT)
            lin = jnp.clip(a_s, -SWIGLU_LIMIT, SWIGLU_LIMIT) + 1.0
            prods.append(gate * _sigmoid(SWIGLU_ALPHA * gate) * pltpu.roll(lin, LANES - 1, 1))
        acts = [jnp.where(even, prods[2 * t], pltpu.roll(prods[2 * t + 1], 1, 1))
                for t in range(D_FF_EXPERT // LANES)]
        act = jnp.concatenate(acts, axis=1).astype(BF16)
        y = jnp.dot(act, w2_ref[...], preferred_element_type=F32) + b2_ref[...]
        for k in range(ROW_CHUNKS):
            o_ref[pl.ds(k, MOE_BLOCK, stride=ROW_CHUNKS), :] = y[:, k * LANES:(k + 1) * LANES]

    @pl.when(i >= n_used)
    def _():
        o_ref[...] = jnp.zeros(o_ref.shape, o_ref.dtype)


def _expert_ffn(h_rows, block_expert, slot_token, n_used, w1, b1, w2, b2, l):
    nb = block_expert.shape[0]
    rows = h_rows.shape[0] // ROW_CHUNKS
    grid_spec = pltpu.PrefetchScalarGridSpec(
        num_scalar_prefetch=3,
        grid=(nb,),
        in_specs=[
            pl.BlockSpec(memory_space=pl.ANY),
            pl.BlockSpec((None, None, D_MODEL, 2 * D_FF_EXPERT), lambda i, be, tok, nu: (l, be[i], 0, 0)),
            pl.BlockSpec((None, None, 1, 2 * D_FF_EXPERT), lambda i, be, tok, nu: (l, be[i], 0, 0)),
            pl.BlockSpec((None, None, D_FF_EXPERT, D_MODEL), lambda i, be, tok, nu: (l, be[i], 0, 0)),
            pl.BlockSpec((None, None, 1, D_MODEL), lambda i, be, tok, nu: (l, be[i], 0, 0)),
        ],
        out_specs=pl.BlockSpec((MOE_BLOCK * ROW_CHUNKS, LANES), lambda i, be, tok, nu: (i, 0)),
        scratch_shapes=[
            pltpu.VMEM((2, MOE_BLOCK * ROW_CHUNKS, LANES), F32),
            pltpu.VMEM((MOE_BLOCK, D_MODEL), BF16),
            pltpu.SemaphoreType.DMA((2,)),
        ],
    )
    return pl.pallas_call(
        _expert_kernel,
        grid_spec=grid_spec,
        out_shape=jax.ShapeDtypeStruct((nb * MOE_BLOCK * ROW_CHUNKS, LANES), F32),
        compiler_params=_cparams("arbitrary"),
    )(block_expert, slot_token, n_used, h_rows.reshape(rows, ROW_CHUNKS, LANES), w1, b1, w2, b2)


COMB_TM = 128


def _combine_kernel(src_ref, y_hbm, x_ref, g_ref, wt_ref, o_ref, cbuf, sem):
    i = pl.program_id(0)
    n = pl.num_programs(0)
    slot = i % 2
    n_rows = COMB_TM * TOP_K

    def start_gather(blk, s):
        base = blk * n_rows

        def issue(r, c):
            pltpu.make_async_copy(y_hbm.at[src_ref[base + r]],
                                  cbuf.at[s, pl.ds(r * ROW_CHUNKS, ROW_CHUNKS), :], sem.at[s]).start()
            return c
        lax.fori_loop(0, n_rows, issue, 0, unroll=8)

    @pl.when(i == 0)
    def _():
        start_gather(0, 0)

    @pl.when(i + 1 < n)
    def _():
        start_gather(i + 1, 1 - slot)

    def wait(r, c):
        pltpu.make_async_copy(y_hbm.at[0], cbuf.at[slot, pl.ds(r * ROW_CHUNKS, ROW_CHUNKS), :],
                              sem.at[slot]).wait()
        return c
    lax.fori_loop(0, n_rows, wait, 0, unroll=8)
    c = cbuf.at[slot]
    wt = wt_ref[...]
    wk = [jnp.broadcast_to(wt[:, kk:kk + 1], (COMB_TM, LANES)) for kk in range(TOP_K)]
    for k in range(ROW_CHUNKS):
        f = wk[0] * c[pl.ds(k, COMB_TM, stride=ROW_CHUNKS), :]
        for kk in range(1, TOP_K):
            f = f + wk[kk] * c[pl.ds(kk * COMB_TM * ROW_CHUNKS + k, COMB_TM, stride=ROW_CHUNKS), :]
        sl = slice(k * LANES, (k + 1) * LANES)
        o_ref[:, sl] = x_ref[:, sl] + g_ref[:, sl] * f


def _moe_combine(src, y_rows, wt_pad, x, mod3, gate_idx):
    rows = x.shape[0]
    n_slots = y_rows.shape[0] // ROW_CHUNKS
    grid_spec = pltpu.PrefetchScalarGridSpec(
        num_scalar_prefetch=1,
        grid=(rows // COMB_TM,),
        in_specs=[
            pl.BlockSpec(memory_space=pl.ANY),
            pl.BlockSpec((COMB_TM, D_MODEL), lambda i, d: (i, 0)),
            pl.BlockSpec((None, 1, D_MODEL), lambda i, d: (_batch_of_block(i, COMB_TM), 0, gate_idx)),
            pl.BlockSpec((COMB_TM, LANES), lambda i, d: (i, 0)),
        ],
        out_specs=pl.BlockSpec((COMB_TM, D_MODEL), lambda i, d: (i, 0)),
        scratch_shapes=[
            pltpu.VMEM((2, COMB_TM * TOP_K * ROW_CHUNKS, LANES), F32),
            pltpu.SemaphoreType.DMA((2,)),
        ],
    )
    return pl.pallas_call(
        _combine_kernel,
        grid_spec=grid_spec,
        out_shape=jax.ShapeDtypeStruct((rows, D_MODEL), F32),
        compiler_params=_cparams("arbitrary"),
    )(src, y_rows.reshape(n_slots, ROW_CHUNKS, LANES), x, mod3, wt_pad)


def _moe_ffn(x, g, mod3, w_router, b_router, w1, b1, w2, b2, l):
    rows = x.shape[0]
    n_assign = rows * TOP_K
    nb = n_assign // MOE_BLOCK + N_EXPERTS
    h_rows, idx_pad, wt_pad, rank_pad, cnt = _norm_router(x, g, mod3, 3, 4, w_router, b_router)
    experts = jnp.arange(N_EXPERTS, dtype=I32)
    counts = cnt[0, :N_EXPERTS].astype(I32)
    padded = (counts + MOE_BLOCK - 1) // MOE_BLOCK * MOE_BLOCK
    padded_end = jnp.cumsum(padded)
    padded_start = padded_end - padded
    idx = idx_pad[:, :TOP_K]
    start_of = jnp.sum(jnp.where(idx[..., None] == experts, padded_start, 0), axis=-1)
    dest = start_of + rank_pad[:, :TOP_K]
    block_start = jnp.arange(nb, dtype=I32) * MOE_BLOCK
    block_expert = jnp.minimum(jnp.sum((block_start[:, None] >= padded_end[None, :]).astype(I32), axis=1),
                               N_EXPERTS - 1)
    n_used = padded_end[-1:] // MOE_BLOCK
    slot_token = jnp.zeros((nb * MOE_BLOCK,), I32).at[dest.reshape(-1)].set(
        jnp.arange(n_assign, dtype=I32) // TOP_K)
    y_rows = _expert_ffn(h_rows, block_expert, slot_token, n_used, w1, b1, w2, b2, l)
    src = dest.reshape(rows // COMB_TM, COMB_TM, TOP_K).transpose(0, 2, 1).reshape(-1)
    return _moe_combine(src, y_rows, wt_pad, x, mod3, 5)


def kernel(x, c, ctx, c_ctx, norm1_g, norm2_g, w_ada, b_ada, w_in, na_rel_bias, q_norm_g, k_norm_g,
           s5_lam_re, s5_lam_im, s5_log_dt, s5_b_re, s5_b_im, s5_c_re, s5_c_im, s5_d, s5_w_glu,
           w_br_a, w_br_s, w_br_c, w_out, w_router, b_router, w_exp1, b_exp1, w_exp2, b_exp2,
           final_norm_g):
    cc = jnp.concatenate([c, c_ctx[None], jnp.zeros((8 - BATCH - 1, D_MODEL), F32)], axis=0)
    mod = _ada_mod(cc, w_ada, b_ada)
    cos, sin = _rope_tables()
    s5w = jax.vmap(_s5_weights)(s5_lam_re, s5_lam_im, s5_log_dt, s5_b_re, s5_b_im,
                                s5_c_re, s5_c_im, s5_d)
    w1b = w_exp1.astype(BF16)
    w2b = _w2_prep(w_exp2)
    b1r = b_exp1.reshape(DEPTH, N_EXPERTS, 1, 2 * D_FF_EXPERT)
    b2r = b_exp2.reshape(DEPTH, N_EXPERTS, 1, D_MODEL)
    xa = jnp.concatenate([x.reshape(N_LAT, D_MODEL), ctx.reshape(N_CTX, D_MODEL)], axis=0)
    for l in range(DEPTH):
        ctx_out = l < DEPTH - 1
        mod3 = mod[l].reshape(8, 1, 6 * D_MODEL)
        h = _norm_mod(xa, norm1_g[l], mod3, 0, 1)
        z = _matmul(h, w_in, l, F32)
        pt = _na_bias_table(na_rel_bias[l])
        ya = _na_attention(z, pt)
        qk = _qk_prep(z, q_norm_g[l], k_norm_g[l], cos, sin)
        yc = _gqa_attention(qk, z)
        ys = _s5_branch(z, s5w, s5_w_glu, l)
        if ctx_out:
            ya_c = _ctx_attention(z, COL_QA // HEAD_DIM, z, COL_KA // HEAD_DIM, z, COL_VA // HEAD_DIM,
                                  NA_HEADS, 1)
            yc_c = _ctx_attention(qk, 0, qk, GQA_Q_WIDTH // HEAD_DIM, z, COL_VG // HEAD_DIM,
                                  GQA_KV_HEADS, GQA_GROUP)
            ya = jnp.concatenate([ya, ya_c], axis=0)
            yc = jnp.concatenate([yc, yc_c], axis=0)
        m = _gated_merge(ya, ys, yc, z, w_br_a, w_br_s, w_br_c, l)
        xa = _proj_residual(m, w_out, l, xa, mod3, 2)
        xa = _moe_ffn(xa, norm2_g[l], mod3, w_router[l], b_router[l], w1b, b1r, w2b, b2r, l)
    return _final_norm(xa[:N_LAT], final_norm_g).reshape(BATCH, SEQ, D_MODEL)
```

```python
import functools
import math

import jax
import jax.numpy as jnp
from jax import lax
from jax.experimental import pallas as pl
from jax.experimental.pallas import tpu as pltpu

F32 = jnp.float32
BF16 = jnp.bfloat16
I32 = jnp.int32

D_MODEL = 2048
BATCH = 4
SEQ = 2048
DEPTH = 2
GRID_W = 64
GRID_H = SEQ // GRID_W
CTX_LEN = 256
HEAD_DIM = 128
NA_HEADS = 8
NA_WIDTH = NA_HEADS * HEAD_DIM
NA_KH = 8
NA_KW = 16
GQA_HEADS = 8
GQA_KV_HEADS = 2
GQA_GROUP = GQA_HEADS // GQA_KV_HEADS
GQA_Q_WIDTH = GQA_HEADS * HEAD_DIM
GQA_KV_WIDTH = GQA_KV_HEADS * HEAD_DIM
ROPE_THETA = 10000.0
S5_WIDTH = 1024
S5_GROUP_CH = 16
S5_GROUPS = S5_WIDTH // S5_GROUP_CH
S5_STATE = 64
N_EXPERTS = 32
TOP_K = 4
D_FF_EXPERT = 1024
SWIGLU_ALPHA = 1.702
SWIGLU_LIMIT = 7.0
RMS_EPS = 1e-6

N_LAT = BATCH * SEQ
N_CTX = BATCH * CTX_LEN
N_ALL = N_LAT + N_CTX
CTX_WIDTH = 2 * NA_WIDTH + 2 * GQA_KV_WIDTH + S5_WIDTH
IN_WIDTH = CTX_WIDTH + NA_WIDTH + GQA_Q_WIDTH + 3 * D_MODEL
COL_KA = 0
COL_VA = NA_WIDTH
COL_KG = 2 * NA_WIDTH
COL_VG = COL_KG + GQA_KV_WIDTH
COL_U = COL_VG + GQA_KV_WIDTH
COL_QA = CTX_WIDTH
COL_QG = COL_QA + NA_WIDTH
COL_GA = COL_QG + GQA_Q_WIDTH
COL_GS = COL_GA + D_MODEL
COL_GC = COL_GS + D_MODEL

LANES = 128
ROW_CHUNKS = D_MODEL // LANES
NEG = -1e30
ATT_SCALE = HEAD_DIM ** -0.5

S5_CHUNK = 16
S5_SEQ = CTX_LEN + SEQ
S5_NCHUNK = S5_SEQ // S5_CHUNK
S5_CTX_CHUNKS = CTX_LEN // S5_CHUNK
S5_ROWS = S5_NCHUNK * BATCH
S5_UNITS = S5_GROUPS // 2
S5_LANES = S5_GROUPS * S5_STATE

MOE_BLOCK = 256
VMEM_LIMIT = 56 * 1024 * 1024


def _cparams(*sem):
    return pltpu.CompilerParams(dimension_semantics=sem, vmem_limit_bytes=VMEM_LIMIT)


def _sigmoid(x):
    return 1.0 / (1.0 + jnp.exp(-x))


def _batch_of_block(i, rows_per_block):
    return jnp.minimum(i // (SEQ // rows_per_block), BATCH)


def _ada_kernel(c_ref, w_ref, b_ref, o_ref):
    c = c_ref[...]
    a = (c * _sigmoid(c)).astype(BF16)
    o_ref[...] = jnp.dot(a, w_ref[...].astype(BF16), preferred_element_type=F32) + b_ref[...]


def _ada_mod(cc, w_ada, b_ada):
    tn = 1024
    return pl.pallas_call(
        _ada_kernel,
        grid=(DEPTH, 6 * D_MODEL // tn),
        in_specs=[
            pl.BlockSpec((8, D_MODEL), lambda l, j: (0, 0)),
            pl.BlockSpec((None, D_MODEL, tn), lambda l, j: (l, 0, j)),
            pl.BlockSpec((None, 1, tn), lambda l, j: (l, 0, j)),
        ],
        out_specs=pl.BlockSpec((None, 8, tn), lambda l, j: (l, 0, j)),
        out_shape=jax.ShapeDtypeStruct((DEPTH, 8, 6 * D_MODEL), F32),
        compiler_params=_cparams("parallel", "parallel"),
    )(cc, w_ada, b_ada.reshape(DEPTH, 1, 6 * D_MODEL))


def _normmod_kernel(x_ref, g_ref, sc_ref, sh_ref, o_ref):
    x = x_ref[...]
    y = x * lax.rsqrt(jnp.mean(x * x, axis=-1, keepdims=True) + RMS_EPS) * g_ref[...]
    o_ref[...] = (y * (1.0 + sc_ref[...]) + sh_ref[...]).astype(o_ref.dtype)


def _norm_mod(x, g, mod3, shift_idx, scale_idx):
    rows = x.shape[0]
    tm = 256
    return pl.pallas_call(
        _normmod_kernel,
        grid=(rows // tm,),
        in_specs=[
            pl.BlockSpec((tm, D_MODEL), lambda i: (i, 0)),
            pl.BlockSpec((1, D_MODEL), lambda i: (0, 0)),
            pl.BlockSpec((None, 1, D_MODEL), lambda i: (_batch_of_block(i, tm), 0, scale_idx)),
            pl.BlockSpec((None, 1, D_MODEL), lambda i: (_batch_of_block(i, tm), 0, shift_idx)),
        ],
        out_specs=pl.BlockSpec((tm, D_MODEL), lambda i: (i, 0)),
        out_shape=jax.ShapeDtypeStruct((rows, D_MODEL), BF16),
        compiler_params=_cparams("parallel"),
    )(x, g.reshape(1, D_MODEL), mod3, mod3)


def _final_norm_kernel(x_ref, g_ref, o_ref):
    x = x_ref[...]
    o_ref[...] = x * lax.rsqrt(jnp.mean(x * x, axis=-1, keepdims=True) + RMS_EPS) * g_ref[...]


def _final_norm(x, g):
    rows = x.shape[0]
    tm = 256
    return pl.pallas_call(
        _final_norm_kernel,
        grid=(rows // tm,),
        in_specs=[pl.BlockSpec((tm, D_MODEL), lambda i: (i, 0)),
                  pl.BlockSpec((1, D_MODEL), lambda i: (0, 0))],
        out_specs=pl.BlockSpec((tm, D_MODEL), lambda i: (i, 0)),
        out_shape=jax.ShapeDtypeStruct((rows, D_MODEL), F32),
        compiler_params=_cparams("parallel"),
    )(x, g.reshape(1, D_MODEL))


def _mm_kernel(a_ref, w_ref, o_ref):
    o_ref[...] = jnp.dot(a_ref[...], w_ref[...].astype(BF16),
                         preferred_element_type=F32).astype(o_ref.dtype)


def _matmul(a, w, l, out_dtype, tm=1024, tn=512):
    m, k = a.shape
    n = w.shape[2]
    return pl.pallas_call(
        _mm_kernel,
        grid=(m // tm, n // tn),
        in_specs=[pl.BlockSpec((tm, k), lambda i, j: (i, 0)),
                  pl.BlockSpec((None, k, tn), lambda i, j: (l, 0, j))],
        out_specs=pl.BlockSpec((tm, tn), lambda i, j: (i, j)),
        out_shape=jax.ShapeDtypeStruct((m, n), out_dtype),
        compiler_params=_cparams("parallel", "parallel"),
    )(a, w)


NA_QROWS = 4
NA_BAND = NA_KH + NA_QROWS
NA_TQ = NA_QROWS * GRID_W
NA_TK = NA_BAND * GRID_W


def _na_bias_table(rel_bias):
    cidx = jnp.arange(GRID_W, dtype=I32)
    col_start = jnp.clip(cidx - NA_KW // 2, 0, GRID_W - NA_KW)
    col_ok = (cidx[None, :] >= col_start[:, None]) & (cidx[None, :] < col_start[:, None] + NA_KW)
    dc = jnp.clip(cidx[None, :] - cidx[:, None] + (NA_KW - 1), 0, 2 * NA_KW - 2)
    t = rel_bias.astype(F32)[:, :, dc]
    t = jnp.where(col_ok[None, None], t, NEG)
    pad = jnp.full((NA_HEADS, 1, GRID_W, GRID_W), NEG, F32)
    t = jnp.concatenate([pad, t, pad], axis=1)
    return jnp.concatenate([t[:, :-1], t[:, 1:]], axis=-1)


def _na_kernel(q_ref, k_ref, v_ref, kc_ref, vc_ref, pt_ref, o_ref):
    i = pl.program_id(2)
    start = jnp.clip(NA_QROWS * i - NA_KH // 2, 0, GRID_H - NA_BAND)
    koff = pl.multiple_of(start * GRID_W, GRID_W * NA_QROWS)
    q = q_ref[...].astype(BF16)
    kb = k_ref[pl.ds(koff, NA_TK), :].astype(BF16)
    vb = v_ref[pl.ds(koff, NA_TK), :].astype(BF16)
    nt = (((1,), (1,)), ((), ()))
    s = lax.dot_general(q, kb, nt, preferred_element_type=F32) * ATT_SCALE
    lane = lax.broadcasted_iota(I32, (GRID_W, 2 * GRID_W), 1)
    bias_rows = []
    for a in range(NA_QROWS):
        rq = NA_QROWS * i + a
        ws = jnp.clip(rq - NA_KH // 2, 0, GRID_H - NA_KH)
        tiles = []
        for jp in range(NA_BAND // 2):
            rk = start + 2 * jp
            e = jnp.clip(rk - rq + NA_KH, 0, 2 * NA_KH - 1)
            ok0 = ((rk >= ws) & (rk < ws + NA_KH)).astype(I32)
            ok1 = ((rk + 1 >= ws) & (rk + 1 < ws + NA_KH)).astype(I32)
            ok = jnp.where(lane < GRID_W, ok0, ok1) > 0
            tiles.append(jnp.where(ok, pt_ref[e], NEG))
        bias_rows.append(jnp.concatenate(tiles, axis=1))
    s = s + jnp.concatenate(bias_rows, axis=0)
    sc = lax.dot_general(q, kc_ref[...].astype(BF16), nt, preferred_element_type=F32) * ATT_SCALE
    m = jnp.maximum(jnp.max(s, axis=-1, keepdims=True), jnp.max(sc, axis=-1, keepdims=True))
    p = jnp.exp(s - m)
    pc = jnp.exp(sc - m)
    den = jnp.sum(p, axis=-1, keepdims=True) + jnp.sum(pc, axis=-1, keepdims=True)
    o = (jnp.dot(p.astype(BF16), vb, preferred_element_type=F32)
         + jnp.dot(pc.astype(BF16), vc_ref[...].astype(BF16), preferred_element_type=F32))
    o_ref[...] = (o / den).astype(o_ref.dtype)


def _na_attention(z, pt):
    qb = SEQ // NA_TQ
    cq, ck, cv = COL_QA // HEAD_DIM, COL_KA // HEAD_DIM, COL_VA // HEAD_DIM
    ctx_blk = N_LAT // CTX_LEN
    return pl.pallas_call(
        _na_kernel,
        grid=(BATCH, NA_HEADS, qb),
        in_specs=[
            pl.BlockSpec((NA_TQ, HEAD_DIM), lambda b, h, i: (b * qb + i, cq + h)),
            pl.BlockSpec((SEQ, HEAD_DIM), lambda b, h, i: (b, ck + h)),
            pl.BlockSpec((SEQ, HEAD_DIM), lambda b, h, i: (b, cv + h)),
            pl.BlockSpec((CTX_LEN, HEAD_DIM), lambda b, h, i: (ctx_blk + b, ck + h)),
            pl.BlockSpec((CTX_LEN, HEAD_DIM), lambda b, h, i: (ctx_blk + b, cv + h)),
            pl.BlockSpec((None, 2 * NA_KH, GRID_W, 2 * GRID_W), lambda b, h, i: (h, 0, 0, 0)),
        ],
        out_specs=pl.BlockSpec((NA_TQ, HEAD_DIM), lambda b, h, i: (b * qb + i, h)),
        out_shape=jax.ShapeDtypeStruct((N_LAT, NA_WIDTH), BF16),
        compiler_params=_cparams("parallel", "parallel", "parallel"),
    )(z, z, z, z, z, pt)


def _softmax_attend(q, ks, vs):
    nt = (((1,), (1,)), ((), ()))
    ss = [lax.dot_general(q, k, nt, preferred_element_type=F32) * ATT_SCALE for k in ks]
    m = ss[0].max(axis=-1, keepdims=True)
    for s in ss[1:]:
        m = jnp.maximum(m, s.max(axis=-1, keepdims=True))
    ps = [jnp.exp(s - m) for s in ss]
    den = ps[0].sum(axis=-1, keepdims=True)
    for p in ps[1:]:
        den = den + p.sum(axis=-1, keepdims=True)
    o = jnp.dot(ps[0].astype(BF16), vs[0], preferred_element_type=F32)
    for p, v in zip(ps[1:], vs[1:]):
        o = o + jnp.dot(p.astype(BF16), v, preferred_element_type=F32)
    return o / den


def _stack_heads(q, group):
    return jnp.concatenate([q[:, g * HEAD_DIM:(g + 1) * HEAD_DIM] for g in range(group)], axis=0)


def _unstack_heads(o, group, rows):
    return jnp.concatenate([o[g * rows:(g + 1) * rows] for g in range(group)], axis=1)


def _ctx_attn_kernel(q_ref, k_ref, v_ref, o_ref, *, group):
    q = _stack_heads(q_ref[...].astype(BF16), group)
    o = _softmax_attend(q, [k_ref[...].astype(BF16)], [v_ref[...].astype(BF16)])
    o_ref[...] = _unstack_heads(o, group, CTX_LEN).astype(o_ref.dtype)


def _ctx_attention(q_arr, q_col, k_arr, k_col, v_arr, v_col, kv_heads, group):
    ctx_blk = N_LAT // CTX_LEN
    qw = group * HEAD_DIM
    return pl.pallas_call(
        functools.partial(_ctx_attn_kernel, group=group),
        grid=(BATCH, kv_heads),
        in_specs=[
            pl.BlockSpec((CTX_LEN, qw), lambda b, h: (ctx_blk + b, q_col // group + h)),
            pl.BlockSpec((CTX_LEN, HEAD_DIM), lambda b, h: (ctx_blk + b, k_col + h)),
            pl.BlockSpec((CTX_LEN, HEAD_DIM), lambda b, h: (ctx_blk + b, v_col + h)),
        ],
        out_specs=pl.BlockSpec((CTX_LEN, qw), lambda b, h: (b, h)),
        out_shape=jax.ShapeDtypeStruct((N_CTX, kv_heads * qw), BF16),
        compiler_params=_cparams("parallel", "parallel"),
    )(q_arr, k_arr, v_arr)


def _rope_tables():
    t = jnp.arange(SEQ, dtype=I32)
    row = (t // GRID_W).astype(F32)
    col = (t % GRID_W).astype(F32)
    n_freq = HEAD_DIM // 4
    inv_freq = ROPE_THETA ** (-jnp.arange(n_freq, dtype=F32) / n_freq)
    ar = row[:, None] * inv_freq
    ac = col[:, None] * inv_freq
    cos = jnp.concatenate([jnp.cos(ar), jnp.cos(ar), jnp.cos(ac), jnp.cos(ac)], axis=-1)
    sin = jnp.concatenate([-jnp.sin(ar), jnp.sin(ar), -jnp.sin(ac), jnp.sin(ac)], axis=-1)
    cos = jnp.concatenate([cos, jnp.ones((CTX_LEN, HEAD_DIM), F32)], axis=0)
    sin = jnp.concatenate([sin, jnp.zeros((CTX_LEN, HEAD_DIM), F32)], axis=0)
    return cos, sin


def _qkprep_kernel(q0_ref, q1_ref, k_ref, cos_ref, sin_ref, qg_ref, kg_ref, o_ref):
    c = cos_ref[...]
    s = sin_ref[...]
    lane = lax.broadcasted_iota(I32, c.shape, 1)
    first = (lane % (HEAD_DIM // 2)) < (HEAD_DIM // 4)

    def prep(x, g):
        y = x * lax.rsqrt(jnp.mean(x * x, axis=-1, keepdims=True) + RMS_EPS) * g
        partner = jnp.where(first, pltpu.roll(y, HEAD_DIM - HEAD_DIM // 4, 1),
                            pltpu.roll(y, HEAD_DIM // 4, 1))
        return (y * c + partner * s).astype(o_ref.dtype)

    half = GQA_HEADS // 2
    for h in range(half):
        sl = slice(h * HEAD_DIM, (h + 1) * HEAD_DIM)
        o_ref[:, sl] = prep(q0_ref[:, sl], qg_ref[...])
        o_ref[:, (half + h) * HEAD_DIM:(half + h + 1) * HEAD_DIM] = prep(q1_ref[:, sl], qg_ref[...])
    for h in range(GQA_KV_HEADS):
        sl = slice(h * HEAD_DIM, (h + 1) * HEAD_DIM)
        o_ref[:, (GQA_HEADS + h) * HEAD_DIM:(GQA_HEADS + h + 1) * HEAD_DIM] = prep(k_ref[:, sl], kg_ref[...])


def _qk_prep(z, q_norm_g, k_norm_g, cos, sin):
    tm = 256
    hq = GQA_Q_WIDTH // 2
    lat_blocks = N_LAT // tm
    per_batch = SEQ // tm
    tbl = lambda i: (jnp.where(i < lat_blocks, i % per_batch, per_batch), 0)
    return pl.pallas_call(
        _qkprep_kernel,
        grid=(N_ALL // tm,),
        in_specs=[
            pl.BlockSpec((tm, hq), lambda i: (i, COL_QG // hq)),
            pl.BlockSpec((tm, hq), lambda i: (i, COL_QG // hq + 1)),
            pl.BlockSpec((tm, GQA_KV_WIDTH), lambda i: (i, COL_KG // GQA_KV_WIDTH)),
            pl.BlockSpec((tm, HEAD_DIM), tbl),
            pl.BlockSpec((tm, HEAD_DIM), tbl),
            pl.BlockSpec((1, HEAD_DIM), lambda i: (0, 0)),
            pl.BlockSpec((1, HEAD_DIM), lambda i: (0, 0)),
        ],
        out_specs=pl.BlockSpec((tm, GQA_Q_WIDTH + GQA_KV_WIDTH), lambda i: (i, 0)),
        out_shape=jax.ShapeDtypeStruct((N_ALL, GQA_Q_WIDTH + GQA_KV_WIDTH), BF16),
        compiler_params=_cparams("parallel"),
    )(z, z, z, cos, sin, q_norm_g.reshape(1, HEAD_DIM), k_norm_g.reshape(1, HEAD_DIM))


GQA_TQ = 256


def _gqa_kernel(q_ref, k_ref, kc_ref, v_ref, vc_ref, o_ref):
    q = _stack_heads(q_ref[...], GQA_GROUP)
    o = _softmax_attend(q, [k_ref[...], kc_ref[...]],
                        [v_ref[...].astype(BF16), vc_ref[...].astype(BF16)])
    o_ref[...] = _unstack_heads(o, GQA_GROUP, GQA_TQ).astype(o_ref.dtype)


def _gqa_attention(qk, z):
    qb = SEQ // GQA_TQ
    qw = GQA_GROUP * HEAD_DIM
    kcol = GQA_Q_WIDTH // HEAD_DIM
    vcol = COL_VG // HEAD_DIM
    ctx_blk = N_LAT // CTX_LEN
    return pl.pallas_call(
        _gqa_kernel,
        grid=(BATCH, GQA_KV_HEADS, qb),
        in_specs=[
            pl.BlockSpec((GQA_TQ, qw), lambda b, h, i: (b * qb + i, h)),
            pl.BlockSpec((SEQ, HEAD_DIM), lambda b, h, i: (b, kcol + h)),
            pl.BlockSpec((CTX_LEN, HEAD_DIM), lambda b, h, i: (ctx_blk + b, kcol + h)),
            pl.BlockSpec((SEQ, HEAD_DIM), lambda b, h, i: (b, vcol + h)),
            pl.BlockSpec((CTX_LEN, HEAD_DIM), lambda b, h, i: (ctx_blk + b, vcol + h)),
        ],
        out_specs=pl.BlockSpec((GQA_TQ, qw), lambda b, h, i: (b * qb + i, h)),
        out_shape=jax.ShapeDtypeStruct((N_LAT, GQA_Q_WIDTH), BF16),
        compiler_params=_cparams("parallel", "parallel", "parallel"),
    )(qk, qk, qk, z, z)


def _pair_blockdiag(x):
    g, a, b = x.shape
    x2 = x.reshape(g // 2, 2, a, b)
    eye = jnp.eye(2, dtype=x.dtype)
    return jnp.einsum('qiab,ij->qiajb', x2, eye).reshape(g // 2, 2 * a, 2 * b)


def _s5_weights(lam_re, lam_im, log_dt, b_re, b_im, c_re, c_im, d_skip):
    hi = lax.Precision.HIGHEST
    G, N, P, C = S5_GROUPS, S5_STATE, S5_GROUP_CH, S5_CHUNK
    lam_re, lam_im = lam_re.astype(F32), lam_im.astype(F32)
    dt = jnp.exp(log_dt.astype(F32))[..., None]
    tau = jnp.arange(C + 1, dtype=F32)[None, None, :, None]
    mag = jnp.exp(lam_re[:, :, None, :] * dt[:, :, None, :] * tau)
    ang = lam_im[:, :, None, :] * dt[:, :, None, :] * tau
    lp_re, lp_im = mag * jnp.cos(ang), mag * jnp.sin(ang)
    nr, ni = lp_re[:, :, 1] - 1.0, lp_im[:, :, 1]
    den = lam_re * lam_re + lam_im * lam_im
    fr = (nr * lam_re + ni * lam_im) / den
    fi = (ni * lam_re - nr * lam_im) / den
    bb_re = fr[..., None] * b_re - fi[..., None] * b_im
    bb_im = fr[..., None] * b_im + fi[..., None] * b_re
    c_re, c_im = c_re.astype(F32), c_im.astype(F32)
    lb_re = lp_re[..., None] * bb_re[:, :, None] - lp_im[..., None] * bb_im[:, :, None]
    lb_im = lp_re[..., None] * bb_im[:, :, None] + lp_im[..., None] * bb_re[:, :, None]
    kern = (jnp.einsum('dgon,dgtni->dgtoi', c_re, lb_re[:, :, :C], precision=hi)
            - jnp.einsum('dgon,dgtni->dgtoi', c_im, lb_im[:, :, :C], precision=hi))
    eye_p = jnp.eye(P, dtype=F32)
    kern = kern.at[0, :, 0].add(eye_p[None] * d_skip.astype(F32).reshape(G, P, 1))
    k2 = kern.transpose(0, 1, 4, 2, 3).reshape(2, G * P, C * P)
    tau_i = jnp.arange(C)[:, None, None]
    s_i = jnp.arange(C)[None, :, None]
    t_i = jnp.arange(C)[None, None, :]
    shifts = jnp.stack([t_i - s_i == tau_i, s_i - t_i == tau_i]).astype(F32)
    sel = jnp.einsum('dzst,po->dzpsto', shifts, eye_p).reshape(2 * C * P, C * C * P)
    m = jnp.dot(jnp.concatenate([k2[0], k2[1]], axis=1), sel, precision=hi)
    m = m.reshape(G, P, C, C * P).transpose(0, 2, 1, 3).reshape(G, C * P, C * P)
    wsf_re = lb_re[0][:, C - 1::-1][:, :C].transpose(0, 1, 3, 2).reshape(G, C * P, N)
    wsf_im = lb_im[0][:, C - 1::-1][:, :C].transpose(0, 1, 3, 2).reshape(G, C * P, N)
    wsr_re = lb_re[1][:, :C].transpose(0, 1, 3, 2).reshape(G, C * P, N)
    wsr_im = lb_im[1][:, :C].transpose(0, 1, 3, 2).reshape(G, C * P, N)
    def state_out(d, powers_re, powers_im):
        cr, ci = c_re[d][:, None], c_im[d][:, None]
        pr, pi_ = powers_re[:, :, None], powers_im[:, :, None]
        from_re = (cr * pr - ci * pi_).transpose(0, 3, 1, 2).reshape(G, N, C * P)
        from_im = (-(cr * pi_ + ci * pr)).transpose(0, 3, 1, 2).reshape(G, N, C * P)
        return from_re, from_im
    of_re, of_im = state_out(0, lp_re[0][:, 1:C + 1], lp_im[0][:, 1:C + 1])
    or_re, or_im = state_out(1, lp_re[1][:, C:0:-1], lp_im[1][:, C:0:-1])
    w_state = jnp.concatenate([_pair_blockdiag(w) for w in (wsf_re, wsf_im, wsr_re, wsr_im)], axis=-1)
    w_out = jnp.concatenate([_pair_blockdiag(w) for w in (m, of_re, of_im, or_re, or_im)], axis=1)
    lam_c = jnp.stack([lp_re[0][:, C], lp_im[0][:, C], lp_re[1][:, C], lp_im[1][:, C]])
    return w_state.astype(BF16), w_out.astype(BF16), lam_c.reshape(4, 1, S5_LANES)


def _s5_state_kernel(u_ref, w_ref, fre_ref, fim_ref, rre_ref, rim_ref):
    r = jnp.dot(u_ref[...], w_ref[...], preferred_element_type=F32)
    for k, ref in enumerate((fre_ref, fim_ref, rre_ref, rim_ref)):
        ref[...] = r[:, k * LANES:(k + 1) * LANES]


def _s5_local_states(u2, w_state, l):
    uw = 2 * S5_CHUNK * S5_GROUP_CH
    out = jax.ShapeDtypeStruct((S5_ROWS, S5_LANES), F32)
    ospec = pl.BlockSpec((S5_ROWS, LANES), lambda q: (0, q))
    return pl.pallas_call(
        _s5_state_kernel,
        grid=(S5_UNITS,),
        in_specs=[pl.BlockSpec((None, S5_ROWS, uw), lambda q: (q, 0, 0)),
                  pl.BlockSpec((None, None, uw, 4 * LANES), lambda q: (l, q, 0, 0))],
        out_specs=[ospec] * 4,
        out_shape=[out] * 4,
        compiler_params=_cparams("parallel"),
    )(u2, w_state)


S5_SCAN_LANES = 512
S5_PAIR_ROWS = 2 * BATCH


def _s5_scan_kernel(fre_ref, fim_ref, rre_ref, rim_ref, lam_ref,
                    pfre_ref, pfim_ref, prre_ref, prim_ref):
    lam = lam_ref[...]
    shape = (S5_PAIR_ROWS, S5_SCAN_LANES)
    top = lax.broadcasted_iota(I32, shape, 0) < BATCH
    n_pairs = S5_NCHUNK // 2
    ctx_pairs = S5_CTX_CHUNKS // 2

    def cmul_add(ar, ai, sr, si, lr, li):
        return ar * sr - ai * si + lr, ar * si + ai * sr + li

    def half_step(ar, ai, sr, si, lr, li, first_top):
        xr, xi = cmul_add(ar, ai, sr, si, lr, li)
        xr, xi = pltpu.roll(xr, BATCH, 0), pltpu.roll(xi, BATCH, 0)
        keep = top if first_top else ~top
        prev_r, prev_i = jnp.where(keep, sr, xr), jnp.where(keep, si, xi)
        yr, yi = cmul_add(ar, ai, xr, xi, lr, li)
        nr = jnp.where(keep, pltpu.roll(yr, BATCH, 0), yr)
        ni = jnp.where(keep, pltpu.roll(yi, BATCH, 0), yi)
        return prev_r, prev_i, nr, ni

    def body(kk, carry):
        sfr, sfi, srr, sri = carry
        fo = pl.multiple_of(kk * S5_PAIR_ROWS, S5_PAIR_ROWS)
        rp = jnp.where(kk < ctx_pairs, ctx_pairs - 1 - kk, n_pairs - 1 + ctx_pairs - kk)
        ro = pl.multiple_of(rp * S5_PAIR_ROWS, S5_PAIR_ROWS)
        pr, pi_, sfr, sfi = half_step(lam[0], lam[1], sfr, sfi,
                                      fre_ref[pl.ds(fo, S5_PAIR_ROWS), :],
                                      fim_ref[pl.ds(fo, S5_PAIR_ROWS), :], True)
        pfre_ref[pl.ds(fo, S5_PAIR_ROWS), :] = pr
        pfim_ref[pl.ds(fo, S5_PAIR_ROWS), :] = pi_
        pr, pi_, srr, sri = half_step(lam[2], lam[3], srr, sri,
                                      rre_ref[pl.ds(ro, S5_PAIR_ROWS), :],
                                      rim_ref[pl.ds(ro, S5_PAIR_ROWS), :], False)
        prre_ref[pl.ds(ro, S5_PAIR_ROWS), :] = pr
        prim_ref[pl.ds(ro, S5_PAIR_ROWS), :] = pi_
        return sfr, sfi, srr, sri

    zero = jnp.zeros(shape, F32)
    lax.fori_loop(0, n_pairs, body, (zero, zero, zero, zero))


def _s5_chunk_scan(loc, lam_c, l):
    spec = pl.BlockSpec((S5_ROWS, S5_SCAN_LANES), lambda j: (0, j))
    out = jax.ShapeDtypeStruct((S5_ROWS, S5_LANES), F32)
    return pl.pallas_call(
        _s5_scan_kernel,
        grid=(S5_LANES // S5_SCAN_LANES,),
        in_specs=[spec] * 4 + [pl.BlockSpec((None, 4, 1, S5_SCAN_LANES), lambda j: (l, 0, 0, j))],
        out_specs=[spec] * 4,
        out_shape=[out] * 4,
        compiler_params=_cparams("parallel"),
    )(*loc, lam_c)


def _s5_out_kernel(u_ref, fre_ref, fim_ref, rre_ref, rim_ref, w_ref, o_ref):
    lhs = jnp.concatenate([u_ref[...]] + [r[...].astype(BF16)
                                           for r in (fre_ref, fim_ref, rre_ref, rim_ref)], axis=1)
    o_ref[...] = jnp.dot(lhs, w_ref[...], preferred_element_type=F32).astype(o_ref.dtype)


def _s5_outputs(u2, prev, w_out, l):
    uw = 2 * S5_CHUNK * S5_GROUP_CH
    sspec = pl.BlockSpec((S5_ROWS, LANES), lambda q: (0, q))
    return pl.pallas_call(
        _s5_out_kernel,
        grid=(S5_UNITS,),
        in_specs=[pl.BlockSpec((None, S5_ROWS, uw), lambda q: (q, 0, 0))] + [sspec] * 4
                 + [pl.BlockSpec((None, None, uw + 4 * LANES, uw), lambda q: (l, q, 0, 0))],
        out_specs=pl.BlockSpec((None, S5_ROWS, uw), lambda q: (q, 0, 0)),
        out_shape=jax.ShapeDtypeStruct((S5_UNITS, S5_ROWS, uw), BF16),
        compiler_params=_cparams("parallel"),
    )(u2, *prev, w_out)


def _glu_kernel(y_ref, w_ref, o_ref):
    y = y_ref[...].astype(F32)
    g = 0.5 * y * (1.0 + jnp.tanh(math.sqrt(2.0 / math.pi) * (y + 0.044715 * (y * y * y))))
    t = jnp.dot(g.astype(BF16), w_ref[...].astype(BF16), preferred_element_type=F32)
    o_ref[...] = (g * _sigmoid(t)).astype(o_ref.dtype)


def _s5_glu(y, w_glu, l):
    rows = y.shape[0]
    tm = 512
    return pl.pallas_call(
        _glu_kernel,
        grid=(rows // tm,),
        in_specs=[pl.BlockSpec((tm, S5_WIDTH), lambda i: (i, 0)),
                  pl.BlockSpec((None, S5_WIDTH, S5_WIDTH), lambda i: (l, 0, 0))],
        out_specs=pl.BlockSpec((tm, S5_WIDTH), lambda i: (i, 0)),
        out_shape=jax.ShapeDtypeStruct((rows, S5_WIDTH), BF16),
        compiler_params=_cparams("parallel"),
    )(y, w_glu)


def _s5_branch(z, s5w, w_glu, l):
    w_state, w_out, lam_c = s5w
    C, P = S5_CHUNK, S5_GROUP_CH
    u = lax.optimization_barrier(z[:, COL_U:COL_U + S5_WIDTH])
    u_seq = jnp.concatenate([u[N_LAT:].reshape(BATCH, CTX_LEN, S5_WIDTH),
                             u[:N_LAT].reshape(BATCH, SEQ, S5_WIDTH)], axis=1)
    u2 = (u_seq.astype(BF16).reshape(BATCH, S5_NCHUNK, C, S5_UNITS, 2, P)
          .transpose(3, 1, 0, 4, 2, 5).reshape(S5_UNITS, S5_ROWS, 2 * C * P))
    loc = _s5_local_states(u2, w_state, l)
    prev = _s5_chunk_scan(loc, lam_c, l)
    y2 = _s5_outputs(u2, prev, w_out, l)
    y = (y2.reshape(S5_UNITS, S5_NCHUNK, BATCH, 2, C, P)
         .transpose(2, 1, 4, 0, 3, 5).reshape(BATCH, S5_SEQ, S5_WIDTH))
    y_all = jnp.concatenate([y[:, CTX_LEN:].reshape(N_LAT, S5_WIDTH),
                             y[:, :CTX_LEN].reshape(N_CTX, S5_WIDTH)], axis=0)
    return _s5_glu(y_all, w_glu, l)


def _merge_kernel(ya_ref, ys_ref, yc_ref, wa_ref, ws_ref, wc_ref, ga_ref, gs_ref, gc_ref, o_ref):
    def branch(y_ref, w_ref, g_ref):
        return _sigmoid(g_ref[...]) * jnp.dot(y_ref[...], w_ref[...].astype(BF16),
                                              preferred_element_type=F32)
    o_ref[...] = (branch(ya_ref, wa_ref, ga_ref) + branch(ys_ref, ws_ref, gs_ref)
                  + branch(yc_ref, wc_ref, gc_ref)).astype(o_ref.dtype)


def _gated_merge(ya, ys, yc, z, w_br_a, w_br_s, w_br_c, l):
    rows = ya.shape[0]
    tm, tn = 1024, 512
    aspec = pl.BlockSpec((tm, NA_WIDTH), lambda i, j: (i, 0))
    wspec = pl.BlockSpec((None, NA_WIDTH, tn), lambda i, j: (l, 0, j))
    gspec = lambda col: pl.BlockSpec((tm, tn), lambda i, j: (i, col // tn + j))
    return pl.pallas_call(
        _merge_kernel,
        grid=(rows // tm, D_MODEL // tn),
        in_specs=[aspec] * 3 + [wspec] * 3 + [gspec(COL_GA), gspec(COL_GS), gspec(COL_GC)],
        out_specs=pl.BlockSpec((tm, tn), lambda i, j: (i, j)),
        out_shape=jax.ShapeDtypeStruct((rows, D_MODEL), BF16),
        compiler_params=_cparams("parallel", "parallel"),
    )(ya, ys, yc, w_br_a, w_br_s, w_br_c, z, z, z)


def _proj_res_kernel(m_ref, w_ref, x_ref, g_ref, o_ref):
    y = jnp.dot(m_ref[...], w_ref[...].astype(BF16), preferred_element_type=F32)
    o_ref[...] = x_ref[...] + g_ref[...] * y


def _proj_residual(m, w_out, l, x, mod3, gate_idx):
    rows = m.shape[0]
    tm, tn = 1024, 512
    nj = D_MODEL // tn
    return pl.pallas_call(
        _proj_res_kernel,
        grid=(rows // tm, nj),
        in_specs=[
            pl.BlockSpec((tm, D_MODEL), lambda i, j: (i, 0)),
            pl.BlockSpec((None, D_MODEL, tn), lambda i, j: (l, 0, j)),
            pl.BlockSpec((tm, tn), lambda i, j: (i, j)),
            pl.BlockSpec((None, 1, tn), lambda i, j: (_batch_of_block(i, tm), 0, gate_idx * nj + j)),
        ],
        out_specs=pl.BlockSpec((tm, tn), lambda i, j: (i, j)),
        out_shape=jax.ShapeDtypeStruct((rows, D_MODEL), F32),
        compiler_params=_cparams("parallel", "parallel"),
    )(m, w_out, x, mod3)


def _router_kernel(x_ref, g_ref, sc_ref, sh_ref, wr_ref, br_ref, h_ref, idx_ref, wt_ref, rank_ref,
                   cnt_ref, count):
    @pl.when(pl.program_id(0) == 0)
    def _():
        count[...] = jnp.zeros(count.shape, count.dtype)

    x = x_ref[...]
    tm = x.shape[0]
    y = x * lax.rsqrt(jnp.mean(x * x, axis=-1, keepdims=True) + RMS_EPS) * g_ref[...]
    h = y * (1.0 + sc_ref[...]) + sh_ref[...]
    for k in range(ROW_CHUNKS):
        h_ref[pl.ds(k, tm, stride=ROW_CHUNKS), :] = h[:, k * LANES:(k + 1) * LANES]
    logits = jnp.dot(h, wr_ref[...], preferred_element_type=F32,
                     precision=lax.Precision.HIGHEST) + br_ref[...]
    lane = lax.broadcasted_iota(I32, logits.shape, 1)
    vals, idxs = [], []
    for _ in range(TOP_K):
        m = jnp.max(logits, axis=-1, keepdims=True)
        sel = jnp.min(jnp.where(logits == m, lane, LANES), axis=-1, keepdims=True)
        vals.append(m)
        idxs.append(sel)
        logits = jnp.where(lane == sel, NEG, logits)
    es = [jnp.exp(v - vals[0]) for v in vals]
    den = es[0] + es[1] + es[2] + es[3]
    hits = [lane == idxs[k] for k in range(TOP_K)]
    per_expert = hits[0].astype(F32)
    for k in range(1, TOP_K):
        per_expert = per_expert + hits[k].astype(F32)
    tri = (lax.broadcasted_iota(I32, (tm, tm), 1) < lax.broadcasted_iota(I32, (tm, tm), 0)).astype(BF16)
    before = jnp.dot(tri, per_expert.astype(BF16), preferred_element_type=F32) + count[...]
    idx_out = jnp.zeros(lane.shape, I32)
    wt_out = jnp.zeros(lane.shape, F32)
    rank_out = jnp.zeros(lane.shape, I32)
    for k in range(TOP_K):
        rank_k = jnp.sum(jnp.where(hits[k], before, 0.0), axis=-1, keepdims=True).astype(I32)
        idx_out = jnp.where(lane == k, idxs[k], idx_out)
        wt_out = jnp.where(lane == k, es[k] / den, wt_out)
        rank_out = jnp.where(lane == k, rank_k, rank_out)
    idx_ref[...] = idx_out
    wt_ref[...] = wt_out
    rank_ref[...] = rank_out
    count[...] = count[...] + jnp.sum(per_expert, axis=0, keepdims=True)
    cnt_ref[...] = count[...]


def _norm_router(x, g, mod3, shift_idx, scale_idx, w_router, b_router):
    rows = x.shape[0]
    tm = 256
    wr = jnp.zeros((D_MODEL, LANES), F32).at[:, :N_EXPERTS].set(w_router)
    br = jnp.full((1, LANES), NEG, F32).at[0, :N_EXPERTS].set(b_router)
    small = pl.BlockSpec((tm, LANES), lambda i: (i, 0))
    return pl.pallas_call(
        _router_kernel,
        grid=(rows // tm,),
        in_specs=[
            pl.BlockSpec((tm, D_MODEL), lambda i: (i, 0)),
            pl.BlockSpec((1, D_MODEL), lambda i: (0, 0)),
            pl.BlockSpec((None, 1, D_MODEL), lambda i: (_batch_of_block(i, tm), 0, scale_idx)),
            pl.BlockSpec((None, 1, D_MODEL), lambda i: (_batch_of_block(i, tm), 0, shift_idx)),
            pl.BlockSpec((D_MODEL, LANES), lambda i: (0, 0)),
            pl.BlockSpec((1, LANES), lambda i: (0, 0)),
        ],
        out_specs=[pl.BlockSpec((tm * ROW_CHUNKS, LANES), lambda i: (i, 0)), small, small, small,
                   pl.BlockSpec((1, LANES), lambda i: (0, 0))],
        out_shape=[jax.ShapeDtypeStruct((rows * ROW_CHUNKS, LANES), F32),
                   jax.ShapeDtypeStruct((rows, LANES), I32),
                   jax.ShapeDtypeStruct((rows, LANES), F32),
                   jax.ShapeDtypeStruct((rows, LANES), I32),
                   jax.ShapeDtypeStruct((1, LANES), F32)],
        scratch_shapes=[pltpu.VMEM((1, LANES), F32)],
        compiler_params=_cparams("arbitrary"),
    )(x, g.reshape(1, D_MODEL), mod3, mod3, wr, br)


W2_PREP_ROWS = 256


def _w2_prep_kernel(w_ref, o_ref, scr):
    half = LANES // 2
    for k in range(ROW_CHUNKS):
        cols = slice(k * LANES, (k + 1) * LANES)
        s = scr.at[k]
        for t in range(W2_PREP_ROWS // LANES):
            s[pl.ds(t * LANES, half, stride=2), :] = w_ref[t * LANES:t * LANES + half, cols]
            s[pl.ds(t * LANES + 1, half, stride=2), :] = w_ref[t * LANES + half:(t + 1) * LANES, cols]
        o_ref[:, cols] = s[...].astype(o_ref.dtype)


def _w2_prep(w2):
    de = DEPTH * N_EXPERTS
    spec = pl.BlockSpec((None, W2_PREP_ROWS, D_MODEL), lambda e, r: (e, r, 0))
    out = pl.pallas_call(
        _w2_prep_kernel,
        grid=(de, D_FF_EXPERT // W2_PREP_ROWS),
        in_specs=[spec],
        out_specs=spec,
        out_shape=jax.ShapeDtypeStruct((de, D_FF_EXPERT, D_MODEL), BF16),
        scratch_shapes=[pltpu.VMEM((ROW_CHUNKS, W2_PREP_ROWS, LANES), F32)],
        compiler_params=_cparams("parallel", "parallel"),
    )(w2.reshape(de, D_FF_EXPERT, D_MODEL))
    return out.reshape(DEPTH, N_EXPERTS, D_FF_EXPERT, D_MODEL)


def _expert_kernel(be_ref, tok_ref, nused_ref, h_hbm, w1_ref, b1_ref, w2_ref, b2_ref,
                   o_ref, gbuf, xb, sem):
    i = pl.program_id(0)
    n_used = nused_ref[0]
    slot = i % 2

    def start_gather(blk, s):
        base = blk * MOE_BLOCK

        def issue(j, c):
            pltpu.make_async_copy(h_hbm.at[tok_ref[base + j]],
                                  gbuf.at[s, pl.ds(j * ROW_CHUNKS, ROW_CHUNKS), :], sem.at[s]).start()
            return c
        lax.fori_loop(0, MOE_BLOCK, issue, 0, unroll=8)

    def wait_gather(s):
        def wait(j, c):
            pltpu.make_async_copy(h_hbm.at[0], gbuf.at[s, pl.ds(j * ROW_CHUNKS, ROW_CHUNKS), :],
                                  sem.at[s]).wait()
            return c
        lax.fori_loop(0, MOE_BLOCK, wait, 0, unroll=8)

    @pl.when(i == 0)
    def _():
        start_gather(0, 0)

    @pl.when(i + 1 < n_used)
    def _():
        start_gather(i + 1, 1 - slot)

    @pl.when(i < n_used)
    def _():
        wait_gather(slot)
        g = gbuf.at[slot]
        for k in range(ROW_CHUNKS):
            xb[:, k * LANES:(k + 1) * LANES] = g[pl.ds(k, MOE_BLOCK, stride=ROW_CHUNKS), :].astype(BF16)
        a = jnp.dot(xb[...], w1_ref[...], preferred_element_type=F32) + b1_ref[...]
        even = lax.broadcasted_iota(I32, (MOE_BLOCK, LANES), 1) % 2 == 0
        prods = []
        for s in range(2 * D_FF_EXPERT // LANES):
            a_s = a[:, s * LANES:(s + 1) * LANES]
            gate = jnp.minimum(a_s, SWIGLU_LIMIT)
            lin = jnp.clip(a_s, -SWIGLU_LIMIT, SWIGLU_LIMIT) + 1.0
            prods.append(gate * _sigmoid(SWIGLU_ALPHA * gate) * pltpu.roll(lin, LANES - 1, 1))
        acts = [jnp.where(even, prods[2 * t], pltpu.roll(prods[2 * t + 1], 1, 1))
                for t in range(D_FF_EXPERT // LANES)]
        act = jnp.concatenate(acts, axis=1).astype(BF16)
        y = jnp.dot(act, w2_ref[...], preferred_element_type=F32) + b2_ref[...]
        for k in range(ROW_CHUNKS):
            o_ref[pl.ds(k, MOE_BLOCK, stride=ROW_CHUNKS), :] = y[:, k * LANES:(k + 1) * LANES]

    @pl.when(i >= n_used)
    def _():
        o_ref[...] = jnp.zeros(o_ref.shape, o_ref.dtype)


def _expert_ffn(h_rows, block_expert, slot_token, n_used, w1, b1, w2, b2, l):
    nb = block_expert.shape[0]
    rows = h_rows.shape[0] // ROW_CHUNKS
    grid_spec = pltpu.PrefetchScalarGridSpec(
        num_scalar_prefetch=3,
        grid=(nb,),
        in_specs=[
            pl.BlockSpec(memory_space=pl.ANY),
            pl.BlockSpec((None, None, D_MODEL, 2 * D_FF_EXPERT), lambda i, be, tok, nu: (l, be[i], 0, 0)),
            pl.BlockSpec((None, None, 1, 2 * D_FF_EXPERT), lambda i, be, tok, nu: (l, be[i], 0, 0)),
            pl.BlockSpec((None, None, D_FF_EXPERT, D_MODEL), lambda i, be, tok, nu: (l, be[i], 0, 0)),
            pl.BlockSpec((None, None, 1, D_MODEL), lambda i, be, tok, nu: (l, be[i], 0, 0)),
        ],
        out_specs=pl.BlockSpec((MOE_BLOCK * ROW_CHUNKS, LANES), lambda i, be, tok, nu: (i, 0)),
        scratch_shapes=[
            pltpu.VMEM((2, MOE_BLOCK * ROW_CHUNKS, LANES), F32),
            pltpu.VMEM((MOE_BLOCK, D_MODEL), BF16),
            pltpu.SemaphoreType.DMA((2,)),
        ],
    )
    return pl.pallas_call(
        _expert_kernel,
        grid_spec=grid_spec,
        out_shape=jax.ShapeDtypeStruct((nb * MOE_BLOCK * ROW_CHUNKS, LANES), F32),
        compiler_params=_cparams("arbitrary"),
    )(block_expert, slot_token, n_used, h_rows.reshape(rows, ROW_CHUNKS, LANES), w1, b1, w2, b2)


COMB_TM = 128


def _combine_kernel(src_ref, y_hbm, x_ref, g_ref, wt_ref, o_ref, cbuf, sem):
    i = pl.program_id(0)
    n = pl.num_programs(0)
    slot = i % 2
    n_rows = COMB_TM * TOP_K

    def start_gather(blk, s):
        base = blk * n_rows

        def issue(r, c):
            pltpu.make_async_copy(y_hbm.at[src_ref[base + r]],
                                  cbuf.at[s, pl.ds(r * ROW_CHUNKS, ROW_CHUNKS), :], sem.at[s]).start()
            return c
        lax.fori_loop(0, n_rows, issue, 0, unroll=8)

    @pl.when(i == 0)
    def _():
        start_gather(0, 0)

    @pl.when(i + 1 < n)
    def _():
        start_gather(i + 1, 1 - slot)

    def wait(r, c):
        pltpu.make_async_copy(y_hbm.at[0], cbuf.at[slot, pl.ds(r * ROW_CHUNKS, ROW_CHUNKS), :],
                              sem.at[slot]).wait()
        return c
    lax.fori_loop(0, n_rows, wait, 0, unroll=8)
    c = cbuf.at[slot]
    wt = wt_ref[...]
    wk = [jnp.broadcast_to(wt[:, kk:kk + 1], (COMB_TM, LANES)) for kk in range(TOP_K)]
    for k in range(ROW_CHUNKS):
        f = wk[0] * c[pl.ds(k, COMB_TM, stride=ROW_CHUNKS), :]
        for kk in range(1, TOP_K):
            f = f + wk[kk] * c[pl.ds(kk * COMB_TM * ROW_CHUNKS + k, COMB_TM, stride=ROW_CHUNKS), :]
        sl = slice(k * LANES, (k + 1) * LANES)
        o_ref[:, sl] = x_ref[:, sl] + g_ref[:, sl] * f


def _moe_combine(src, y_rows, wt_pad, x, mod3, gate_idx):
    rows = x.shape[0]
    n_slots = y_rows.shape[0] // ROW_CHUNKS
    grid_spec = pltpu.PrefetchScalarGridSpec(
        num_scalar_prefetch=1,
        grid=(rows // COMB_TM,),
        in_specs=[
            pl.BlockSpec(memory_space=pl.ANY),
            pl.BlockSpec((COMB_TM, D_MODEL), lambda i, d: (i, 0)),
            pl.BlockSpec((None, 1, D_MODEL), lambda i, d: (_batch_of_block(i, COMB_TM), 0, gate_idx)),
            pl.BlockSpec((COMB_TM, LANES), lambda i, d: (i, 0)),
        ],
        out_specs=pl.BlockSpec((COMB_TM, D_MODEL), lambda i, d: (i, 0)),
        scratch_shapes=[
            pltpu.VMEM((2, COMB_TM * TOP_K * ROW_CHUNKS, LANES), F32),
            pltpu.SemaphoreType.DMA((2,)),
        ],
    )
    return pl.pallas_call(
        _combine_kernel,
        grid_spec=grid_spec,
        out_shape=jax.ShapeDtypeStruct((rows, D_MODEL), F32),
        compiler_params=_cparams("arbitrary"),
    )(src, y_rows.reshape(n_slots, ROW_CHUNKS, LANES), x, mod3, wt_pad)


def _moe_ffn(x, g, mod3, w_router, b_router, w1, b1, w2, b2, l):
    rows = x.shape[0]
    n_assign = rows * TOP_K
    nb = n_assign // MOE_BLOCK + N_EXPERTS
    h_rows, idx_pad, wt_pad, rank_pad, cnt = _norm_router(x, g, mod3, 3, 4, w_router, b_router)
    experts = jnp.arange(N_EXPERTS, dtype=I32)
    counts = cnt[0, :N_EXPERTS].astype(I32)
    padded = (counts + MOE_BLOCK - 1) // MOE_BLOCK * MOE_BLOCK
    padded_end = jnp.cumsum(padded)
    padded_start = padded_end - padded
    idx = idx_pad[:, :TOP_K]
    start_of = jnp.sum(jnp.where(idx[..., None] == experts, padded_start, 0), axis=-1)
    dest = start_of + rank_pad[:, :TOP_K]
    block_start = jnp.arange(nb, dtype=I32) * MOE_BLOCK
    block_expert = jnp.minimum(jnp.sum((block_start[:, None] >= padded_end[None, :]).astype(I32), axis=1),
                               N_EXPERTS - 1)
    n_used = padded_end[-1:] // MOE_BLOCK
    slot_token = jnp.zeros((nb * MOE_BLOCK,), I32).at[dest.reshape(-1)].set(
        jnp.arange(n_assign, dtype=I32) // TOP_K)
    y_rows = _expert_ffn(h_rows, block_expert, slot_token, n_used, w1, b1, w2, b2, l)
    src = dest.reshape(rows // COMB_TM, COMB_TM, TOP_K).transpose(0, 2, 1).reshape(-1)
    return _moe_combine(src, y_rows, wt_pad, x, mod3, 5)


def kernel(x, c, ctx, c_ctx, norm1_g, norm2_g, w_ada, b_ada, w_in, na_rel_bias, q_norm_g, k_norm_g,
           s5_lam_re, s5_lam_im, s5_log_dt, s5_b_re, s5_b_im, s5_c_re, s5_c_im, s5_d, s5_w_glu,
           w_br_a, w_br_s, w_br_c, w_out, w_router, b_router, w_exp1, b_exp1, w_exp2, b_exp2,
           final_norm_g):
    cc = jnp.concatenate([c, c_ctx[None], jnp.zeros((8 - BATCH - 1, D_MODEL), F32)], axis=0)
    mod = _ada_mod(cc, w_ada, b_ada)
    cos, sin = _rope_tables()
    per_layer = [_s5_weights(s5_lam_re[l], s5_lam_im[l], s5_log_dt[l], s5_b_re[l], s5_b_im[l],
                             s5_c_re[l], s5_c_im[l], s5_d[l]) for l in range(DEPTH)]
    s5w = tuple(jnp.stack(ws) for ws in zip(*per_layer))
    w1b = w_exp1.astype(BF16)
    w2b = _w2_prep(w_exp2)
    b1r = b_exp1.reshape(DEPTH, N_EXPERTS, 1, 2 * D_FF_EXPERT)
    b2r = b_exp2.reshape(DEPTH, N_EXPERTS, 1, D_MODEL)
    xa = jnp.concatenate([x.reshape(N_LAT, D_MODEL), ctx.reshape(N_CTX, D_MODEL)], axis=0)
    for l in range(DEPTH):
        ctx_out = l < DEPTH - 1
        mod3 = mod[l].reshape(8, 1, 6 * D_MODEL)
        h = _norm_mod(xa, norm1_g[l], mod3, 0, 1)
        z = _matmul(h, w_in, l, F32)
        pt = _na_bias_table(na_rel_bias[l])
        ya = _na_attention(z, pt)
        qk = _qk_prep(z, q_norm_g[l], k_norm_g[l], cos, sin)
        yc = _gqa_attention(qk, z)
        ys = _s5_branch(z, s5w, s5_w_glu, l)
        if ctx_out:
            ya_c = _ctx_attention(z, COL_QA // HEAD_DIM, z, COL_KA // HEAD_DIM, z, COL_VA // HEAD_DIM,
                                  NA_HEADS, 1)
            yc_c = _ctx_attention(qk, 0, qk, GQA_Q_WIDTH // HEAD_DIM, z, COL_VG // HEAD_DIM,
                                  GQA_KV_HEADS, GQA_GROUP)
            ya = jnp.concatenate([ya, ya_c], axis=0)
            yc = jnp.concatenate([yc, yc_c], axis=0)
        m = _gated_merge(ya, ys, yc, z, w_br_a, w_br_s, w_br_c, l)
        xa = _proj_residual(m, w_out, l, xa, mod3, 2)
        xa = _moe_ffn(xa, norm2_g[l], mod3, w_router[l], b_router[l], w1b, b1r, w2b, b2r, l)
    return _final_norm(xa[:N_LAT], final_norm_g).reshape(BATCH, SEQ, D_MODEL)
```

```python
import functools
import math

import jax
import jax.numpy as jnp
from jax import lax
from jax.experimental import pallas as pl
from jax.experimental.pallas import tpu as pltpu

F32 = jnp.float32
BF16 = jnp.bfloat16
I32 = jnp.int32

D_MODEL = 2048
BATCH = 4
SEQ = 2048
DEPTH = 2
GRID_W = 64
GRID_H = SEQ // GRID_W
CTX_LEN = 256
HEAD_DIM = 128
NA_HEADS = 8
NA_WIDTH = NA_HEADS * HEAD_DIM
NA_KH = 8
NA_KW = 16
GQA_HEADS = 8
GQA_KV_HEADS = 2
GQA_GROUP = GQA_HEADS // GQA_KV_HEADS
GQA_Q_WIDTH = GQA_HEADS * HEAD_DIM
GQA_KV_WIDTH = GQA_KV_HEADS * HEAD_DIM
ROPE_THETA = 10000.0
S5_WIDTH = 1024
S5_GROUP_CH = 16
S5_GROUPS = S5_WIDTH // S5_GROUP_CH
S5_STATE = 64
N_EXPERTS = 32
TOP_K = 4
D_FF_EXPERT = 1024
SWIGLU_ALPHA = 1.702
SWIGLU_LIMIT = 7.0
RMS_EPS = 1e-6

N_LAT = BATCH * SEQ
N_CTX = BATCH * CTX_LEN
N_ALL = N_LAT + N_CTX
CTX_WIDTH = 2 * NA_WIDTH + 2 * GQA_KV_WIDTH + S5_WIDTH
IN_WIDTH = CTX_WIDTH + NA_WIDTH + GQA_Q_WIDTH + 3 * D_MODEL
COL_KA = 0
COL_VA = NA_WIDTH
COL_KG = 2 * NA_WIDTH
COL_VG = COL_KG + GQA_KV_WIDTH
COL_U = COL_VG + GQA_KV_WIDTH
COL_QA = CTX_WIDTH
COL_QG = COL_QA + NA_WIDTH
COL_GA = COL_QG + GQA_Q_WIDTH
COL_GS = COL_GA + D_MODEL
COL_GC = COL_GS + D_MODEL

LANES = 128
ROW_CHUNKS = D_MODEL // LANES
NEG = -1e30
ATT_SCALE = HEAD_DIM ** -0.5

S5_CHUNK = 16
S5_SEQ = CTX_LEN + SEQ
S5_NCHUNK = S5_SEQ // S5_CHUNK
S5_CTX_CHUNKS = CTX_LEN // S5_CHUNK
S5_ROWS = S5_NCHUNK * BATCH
S5_UNITS = S5_GROUPS // 2
S5_LANES = S5_GROUPS * S5_STATE

MOE_BLOCK = 256
VMEM_LIMIT = 56 * 1024 * 1024


def _cparams(*sem):
    return pltpu.CompilerParams(dimension_semantics=sem, vmem_limit_bytes=VMEM_LIMIT)


def _sigmoid(x):
    return 1.0 / (1.0 + jnp.exp(-x))


def _batch_of_block(i, rows_per_block):
    return jnp.minimum(i // (SEQ // rows_per_block), BATCH)


def _ada_kernel(c_ref, w_ref, b_ref, o_ref):
    c = c_ref[...]
    a = (c * _sigmoid(c)).astype(BF16)
    o_ref[...] = jnp.dot(a, w_ref[...].astype(BF16), preferred_element_type=F32) + b_ref[...]


def _ada_mod(cc, w_ada, b_ada):
    tn = 1024
    return pl.pallas_call(
        _ada_kernel,
        grid=(DEPTH, 6 * D_MODEL // tn),
        in_specs=[
            pl.BlockSpec((8, D_MODEL), lambda l, j: (0, 0)),
            pl.BlockSpec((None, D_MODEL, tn), lambda l, j: (l, 0, j)),
            pl.BlockSpec((None, 1, tn), lambda l, j: (l, 0, j)),
        ],
        out_specs=pl.BlockSpec((None, 8, tn), lambda l, j: (l, 0, j)),
        out_shape=jax.ShapeDtypeStruct((DEPTH, 8, 6 * D_MODEL), F32),
        compiler_params=_cparams("parallel", "parallel"),
    )(cc, w_ada, b_ada.reshape(DEPTH, 1, 6 * D_MODEL))


def _normmod_kernel(x_ref, g_ref, sc_ref, sh_ref, o_ref):
    x = x_ref[...]
    y = x * lax.rsqrt(jnp.mean(x * x, axis=-1, keepdims=True) + RMS_EPS) * g_ref[...]
    o_ref[...] = (y * (1.0 + sc_ref[...]) + sh_ref[...]).astype(o_ref.dtype)


def _norm_mod(x, g, mod3, shift_idx, scale_idx):
    rows = x.shape[0]
    tm = 256
    return pl.pallas_call(
        _normmod_kernel,
        grid=(rows // tm,),
        in_specs=[
            pl.BlockSpec((tm, D_MODEL), lambda i: (i, 0)),
            pl.BlockSpec((1, D_MODEL), lambda i: (0, 0)),
            pl.BlockSpec((None, 1, D_MODEL), lambda i: (_batch_of_block(i, tm), 0, scale_idx)),
            pl.BlockSpec((None, 1, D_MODEL), lambda i: (_batch_of_block(i, tm), 0, shift_idx)),
        ],
        out_specs=pl.BlockSpec((tm, D_MODEL), lambda i: (i, 0)),
        out_shape=jax.ShapeDtypeStruct((rows, D_MODEL), BF16),
        compiler_params=_cparams("parallel"),
    )(x, g.reshape(1, D_MODEL), mod3, mod3)


def _final_norm_kernel(x_ref, g_ref, o_ref):
    x = x_ref[...]
    o_ref[...] = x * lax.rsqrt(jnp.mean(x * x, axis=-1, keepdims=True) + RMS_EPS) * g_ref[...]


def _final_norm(x, g):
    rows = x.shape[0]
    tm = 256
    return pl.pallas_call(
        _final_norm_kernel,
        grid=(rows // tm,),
        in_specs=[pl.BlockSpec((tm, D_MODEL), lambda i: (i, 0)),
                  pl.BlockSpec((1, D_MODEL), lambda i: (0, 0))],
        out_specs=pl.BlockSpec((tm, D_MODEL), lambda i: (i, 0)),
        out_shape=jax.ShapeDtypeStruct((rows, D_MODEL), F32),
        compiler_params=_cparams("parallel"),
    )(x, g.reshape(1, D_MODEL))


def _mm_kernel(a_ref, w_ref, o_ref):
    o_ref[...] = jnp.dot(a_ref[...], w_ref[...].astype(BF16),
                         preferred_element_type=F32).astype(o_ref.dtype)


def _matmul(a, w, l, out_dtype, tm=1024, tn=512):
    m, k = a.shape
    n = w.shape[2]
    return pl.pallas_call(
        _mm_kernel,
        grid=(m // tm, n // tn),
        in_specs=[pl.BlockSpec((tm, k), lambda i, j: (i, 0)),
                  pl.BlockSpec((None, k, tn), lambda i, j: (l, 0, j))],
        out_specs=pl.BlockSpec((tm, tn), lambda i, j: (i, j)),
        out_shape=jax.ShapeDtypeStruct((m, n), out_dtype),
        compiler_params=_cparams("parallel", "parallel"),
    )(a, w)


NA_QROWS = 4
NA_BAND = NA_KH + NA_QROWS
NA_TQ = NA_QROWS * GRID_W
NA_TK = NA_BAND * GRID_W


def _na_bias_table(rel_bias):
    cidx = jnp.arange(GRID_W, dtype=I32)
    col_start = jnp.clip(cidx - NA_KW // 2, 0, GRID_W - NA_KW)
    col_ok = (cidx[None, :] >= col_start[:, None]) & (cidx[None, :] < col_start[:, None] + NA_KW)
    dc = jnp.clip(cidx[None, :] - cidx[:, None] + (NA_KW - 1), 0, 2 * NA_KW - 2)
    t = rel_bias.astype(F32)[:, :, dc]
    t = jnp.where(col_ok[None, None], t, NEG)
    pad = jnp.full((NA_HEADS, 1, GRID_W, GRID_W), NEG, F32)
    t = jnp.concatenate([pad, t, pad], axis=1)
    return jnp.concatenate([t[:, :-1], t[:, 1:]], axis=-1)


def _na_kernel(q_ref, k_ref, v_ref, kc_ref, vc_ref, pt_ref, o_ref):
    i = pl.program_id(2)
    start = jnp.clip(NA_QROWS * i - NA_KH // 2, 0, GRID_H - NA_BAND)
    koff = pl.multiple_of(start * GRID_W, GRID_W * NA_QROWS)
    q = q_ref[...].astype(BF16)
    kb = k_ref[pl.ds(koff, NA_TK), :].astype(BF16)
    vb = v_ref[pl.ds(koff, NA_TK), :].astype(BF16)
    nt = (((1,), (1,)), ((), ()))
    s = lax.dot_general(q, kb, nt, preferred_element_type=F32) * ATT_SCALE
    lane = lax.broadcasted_iota(I32, (GRID_W, 2 * GRID_W), 1)
    bias_rows = []
    for a in range(NA_QROWS):
        rq = NA_QROWS * i + a
        ws = jnp.clip(rq - NA_KH // 2, 0, GRID_H - NA_KH)
        tiles = []
        for jp in range(NA_BAND // 2):
            rk = start + 2 * jp
            e = jnp.clip(rk - rq + NA_KH, 0, 2 * NA_KH - 1)
            ok0 = ((rk >= ws) & (rk < ws + NA_KH)).astype(I32)
            ok1 = ((rk + 1 >= ws) & (rk + 1 < ws + NA_KH)).astype(I32)
            ok = jnp.where(lane < GRID_W, ok0, ok1) > 0
            tiles.append(jnp.where(ok, pt_ref[e], NEG))
        bias_rows.append(jnp.concatenate(tiles, axis=1))
    s = s + jnp.concatenate(bias_rows, axis=0)
    sc = lax.dot_general(q, kc_ref[...].astype(BF16), nt, preferred_element_type=F32) * ATT_SCALE
    m = jnp.maximum(jnp.max(s, axis=-1, keepdims=True), jnp.max(sc, axis=-1, keepdims=True))
    p = jnp.exp(s - m)
    pc = jnp.exp(sc - m)
    den = jnp.sum(p, axis=-1, keepdims=True) + jnp.sum(pc, axis=-1, keepdims=True)
    o = (jnp.dot(p.astype(BF16), vb, preferred_element_type=F32)
         + jnp.dot(pc.astype(BF16), vc_ref[...].astype(BF16), preferred_element_type=F32))
    o_ref[...] = (o / den).astype(o_ref.dtype)


def _na_attention(z, pt):
    qb = SEQ // NA_TQ
    cq, ck, cv = COL_QA // HEAD_DIM, COL_KA // HEAD_DIM, COL_VA // HEAD_DIM
    ctx_blk = N_LAT // CTX_LEN
    return pl.pallas_call(
        _na_kernel,
        grid=(BATCH, NA_HEADS, qb),
        in_specs=[
            pl.BlockSpec((NA_TQ, HEAD_DIM), lambda b, h, i: (b * qb + i, cq + h)),
            pl.BlockSpec((SEQ, HEAD_DIM), lambda b, h, i: (b, ck + h)),
            pl.BlockSpec((SEQ, HEAD_DIM), lambda b, h, i: (b, cv + h)),
            pl.BlockSpec((CTX_LEN, HEAD_DIM), lambda b, h, i: (ctx_blk + b, ck + h)),
            pl.BlockSpec((CTX_LEN, HEAD_DIM), lambda b, h, i: (ctx_blk + b, cv + h)),
            pl.BlockSpec((None, 2 * NA_KH, GRID_W, 2 * GRID_W), lambda b, h, i: (h, 0, 0, 0)),
        ],
        out_specs=pl.BlockSpec((NA_TQ, HEAD_DIM), lambda b, h, i: (b * qb + i, h)),
        out_shape=jax.ShapeDtypeStruct((N_LAT, NA_WIDTH), BF16),
        compiler_params=_cparams("parallel", "parallel", "parallel"),
    )(z, z, z, z, z, pt)


def _softmax_attend(q, ks, vs):
    nt = (((1,), (1,)), ((), ()))
    ss = [lax.dot_general(q, k, nt, preferred_element_type=F32) * ATT_SCALE for k in ks]
    m = ss[0].max(axis=-1, keepdims=True)
    for s in ss[1:]:
        m = jnp.maximum(m, s.max(axis=-1, keepdims=True))
    ps = [jnp.exp(s - m) for s in ss]
    den = ps[0].sum(axis=-1, keepdims=True)
    for p in ps[1:]:
        den = den + p.sum(axis=-1, keepdims=True)
    o = jnp.dot(ps[0].astype(BF16), vs[0], preferred_element_type=F32)
    for p, v in zip(ps[1:], vs[1:]):
        o = o + jnp.dot(p.astype(BF16), v, preferred_element_type=F32)
    return o / den


def _stack_heads(q, group):
    return jnp.concatenate([q[:, g * HEAD_DIM:(g + 1) * HEAD_DIM] for g in range(group)], axis=0)


def _unstack_heads(o, group, rows):
    return jnp.concatenate([o[g * rows:(g + 1) * rows] for g in range(group)], axis=1)


def _ctx_attn_kernel(q_ref, k_ref, v_ref, o_ref, *, group):
    q = _stack_heads(q_ref[...].astype(BF16), group)
    o = _softmax_attend(q, [k_ref[...].astype(BF16)], [v_ref[...].astype(BF16)])
    o_ref[...] = _unstack_heads(o, group, CTX_LEN).astype(o_ref.dtype)


def _ctx_attention(q_arr, q_col, k_arr, k_col, v_arr, v_col, kv_heads, group):
    ctx_blk = N_LAT // CTX_LEN
    qw = group * HEAD_DIM
    return pl.pallas_call(
        functools.partial(_ctx_attn_kernel, group=group),
        grid=(BATCH, kv_heads),
        in_specs=[
            pl.BlockSpec((CTX_LEN, qw), lambda b, h: (ctx_blk + b, q_col // group + h)),
            pl.BlockSpec((CTX_LEN, HEAD_DIM), lambda b, h: (ctx_blk + b, k_col + h)),
            pl.BlockSpec((CTX_LEN, HEAD_DIM), lambda b, h: (ctx_blk + b, v_col + h)),
        ],
        out_specs=pl.BlockSpec((CTX_LEN, qw), lambda b, h: (b, h)),
        out_shape=jax.ShapeDtypeStruct((N_CTX, kv_heads * qw), BF16),
        compiler_params=_cparams("parallel", "parallel"),
    )(q_arr, k_arr, v_arr)


def _rope_tables():
    t = jnp.arange(SEQ, dtype=I32)
    row = (t // GRID_W).astype(F32)
    col = (t % GRID_W).astype(F32)
    n_freq = HEAD_DIM // 4
    inv_freq = ROPE_THETA ** (-jnp.arange(n_freq, dtype=F32) / n_freq)
    ar = row[:, None] * inv_freq
    ac = col[:, None] * inv_freq
    cos = jnp.concatenate([jnp.cos(ar), jnp.cos(ar), jnp.cos(ac), jnp.cos(ac)], axis=-1)
    sin = jnp.concatenate([-jnp.sin(ar), jnp.sin(ar), -jnp.sin(ac), jnp.sin(ac)], axis=-1)
    cos = jnp.concatenate([cos, jnp.ones((CTX_LEN, HEAD_DIM), F32)], axis=0)
    sin = jnp.concatenate([sin, jnp.zeros((CTX_LEN, HEAD_DIM), F32)], axis=0)
    return cos, sin


def _qkprep_kernel(q0_ref, q1_ref, k_ref, cos_ref, sin_ref, qg_ref, kg_ref, o_ref):
    c = cos_ref[...]
    s = sin_ref[...]
    lane = lax.broadcasted_iota(I32, c.shape, 1)
    first = (lane % (HEAD_DIM // 2)) < (HEAD_DIM // 4)

    def prep(x, g):
        y = x * lax.rsqrt(jnp.mean(x * x, axis=-1, keepdims=True) + RMS_EPS) * g
        partner = jnp.where(first, pltpu.roll(y, HEAD_DIM - HEAD_DIM // 4, 1),
                            pltpu.roll(y, HEAD_DIM // 4, 1))
        return (y * c + partner * s).astype(o_ref.dtype)

    half = GQA_HEADS // 2
    for h in range(half):
        sl = slice(h * HEAD_DIM, (h + 1) * HEAD_DIM)
        o_ref[:, sl] = prep(q0_ref[:, sl], qg_ref[...])
        o_ref[:, (half + h) * HEAD_DIM:(half + h + 1) * HEAD_DIM] = prep(q1_ref[:, sl], qg_ref[...])
    for h in range(GQA_KV_HEADS):
        sl = slice(h * HEAD_DIM, (h + 1) * HEAD_DIM)
        o_ref[:, (GQA_HEADS + h) * HEAD_DIM:(GQA_HEADS + h + 1) * HEAD_DIM] = prep(k_ref[:, sl], kg_ref[...])


def _qk_prep(z, q_norm_g, k_norm_g, cos, sin):
    tm = 256
    hq = GQA_Q_WIDTH // 2
    lat_blocks = N_LAT // tm
    per_batch = SEQ // tm
    tbl = lambda i: (jnp.where(i < lat_blocks, i % per_batch, per_batch), 0)
    return pl.pallas_call(
        _qkprep_kernel,
        grid=(N_ALL // tm,),
        in_specs=[
            pl.BlockSpec((tm, hq), lambda i: (i, COL_QG // hq)),
            pl.BlockSpec((tm, hq), lambda i: (i, COL_QG // hq + 1)),
            pl.BlockSpec((tm, GQA_KV_WIDTH), lambda i: (i, COL_KG // GQA_KV_WIDTH)),
            pl.BlockSpec((tm, HEAD_DIM), tbl),
            pl.BlockSpec((tm, HEAD_DIM), tbl),
            pl.BlockSpec((1, HEAD_DIM), lambda i: (0, 0)),
            pl.BlockSpec((1, HEAD_DIM), lambda i: (0, 0)),
        ],
        out_specs=pl.BlockSpec((tm, GQA_Q_WIDTH + GQA_KV_WIDTH), lambda i: (i, 0)),
        out_shape=jax.ShapeDtypeStruct((N_ALL, GQA_Q_WIDTH + GQA_KV_WIDTH), BF16),
        compiler_params=_cparams("parallel"),
    )(z, z, z, cos, sin, q_norm_g.reshape(1, HEAD_DIM), k_norm_g.reshape(1, HEAD_DIM))


GQA_TQ = 256


def _gqa_kernel(q_ref, k_ref, kc_ref, v_ref, vc_ref, o_ref):
    q = _stack_heads(q_ref[...], GQA_GROUP)
    o = _softmax_attend(q, [k_ref[...], kc_ref[...]],
                        [v_ref[...].astype(BF16), vc_ref[...].astype(BF16)])
    o_ref[...] = _unstack_heads(o, GQA_GROUP, GQA_TQ).astype(o_ref.dtype)


def _gqa_attention(qk, z):
    qb = SEQ // GQA_TQ
    qw = GQA_GROUP * HEAD_DIM
    kcol = GQA_Q_WIDTH // HEAD_DIM
    vcol = COL_VG // HEAD_DIM
    ctx_blk = N_LAT // CTX_LEN
    return pl.pallas_call(
        _gqa_kernel,
        grid=(BATCH, GQA_KV_HEADS, qb),
        in_specs=[
            pl.BlockSpec((GQA_TQ, qw), lambda b, h, i: (b * qb + i, h)),
            pl.BlockSpec((SEQ, HEAD_DIM), lambda b, h, i: (b, kcol + h)),
            pl.BlockSpec((CTX_LEN, HEAD_DIM), lambda b, h, i: (ctx_blk + b, kcol + h)),
            pl.BlockSpec((SEQ, HEAD_DIM), lambda b, h, i: (b, vcol + h)),
            pl.BlockSpec((CTX_LEN, HEAD_DIM), lambda b, h, i: (ctx_blk + b, vcol + h)),
        ],
        out_specs=pl.BlockSpec((GQA_TQ, qw), lambda b, h, i: (b * qb + i, h)),
        out_shape=jax.ShapeDtypeStruct((N_LAT, GQA_Q_WIDTH), BF16),
        compiler_params=_cparams("parallel", "parallel", "parallel"),
    )(qk, qk, qk, z, z)


def _pair_blockdiag(x):
    g, a, b = x.shape
    x2 = x.reshape(g // 2, 2, a, b)
    eye = jnp.eye(2, dtype=x.dtype)
    return jnp.einsum('qiab,ij->qiajb', x2, eye).reshape(g // 2, 2 * a, 2 * b)


def _s5_weights(lam_re, lam_im, log_dt, b_re, b_im, c_re, c_im, d_skip):
    hi = lax.Precision.HIGHEST
    G, N, P, C = S5_GROUPS, S5_STATE, S5_GROUP_CH, S5_CHUNK
    lam_re, lam_im = lam_re.astype(F32), lam_im.astype(F32)
    dt = jnp.exp(log_dt.astype(F32))[..., None]
    tau = jnp.arange(C + 1, dtype=F32)[None, None, :, None]
    mag = jnp.exp(lam_re[:, :, None, :] * dt[:, :, None, :] * tau)
    ang = lam_im[:, :, None, :] * dt[:, :, None, :] * tau
    lp_re, lp_im = mag * jnp.cos(ang), mag * jnp.sin(ang)
    nr, ni = lp_re[:, :, 1] - 1.0, lp_im[:, :, 1]
    den = lam_re * lam_re + lam_im * lam_im
    fr = (nr * lam_re + ni * lam_im) / den
    fi = (ni * lam_re - nr * lam_im) / den
    bb_re = fr[..., None] * b_re - fi[..., None] * b_im
    bb_im = fr[..., None] * b_im + fi[..., None] * b_re
    c_re, c_im = c_re.astype(F32), c_im.astype(F32)
    lb_re = lp_re[..., None] * bb_re[:, :, None] - lp_im[..., None] * bb_im[:, :, None]
    lb_im = lp_re[..., None] * bb_im[:, :, None] + lp_im[..., None] * bb_re[:, :, None]
    kern = (jnp.einsum('dgon,dgtni->dgtoi', c_re, lb_re[:, :, :C], precision=hi)
            - jnp.einsum('dgon,dgtni->dgtoi', c_im, lb_im[:, :, :C], precision=hi))
    eye_p = jnp.eye(P, dtype=F32)
    kern = kern.at[0, :, 0].add(eye_p[None] * d_skip.astype(F32).reshape(G, P, 1))
    k2 = kern.transpose(0, 1, 4, 2, 3).reshape(2, G * P, C * P)
    tau_i = jnp.arange(C)[:, None, None]
    s_i = jnp.arange(C)[None, :, None]
    t_i = jnp.arange(C)[None, None, :]
    shifts = jnp.stack([t_i - s_i == tau_i, s_i - t_i == tau_i]).astype(F32)
    sel = jnp.einsum('dzst,po->dzpsto', shifts, eye_p).reshape(2 * C * P, C * C * P)
    m = jnp.dot(jnp.concatenate([k2[0], k2[1]], axis=1), sel, precision=hi)
    m = m.reshape(G, P, C, C * P).transpose(0, 2, 1, 3).reshape(G, C * P, C * P)
    wsf_re = lb_re[0][:, C - 1::-1][:, :C].transpose(0, 1, 3, 2).reshape(G, C * P, N)
    wsf_im = lb_im[0][:, C - 1::-1][:, :C].transpose(0, 1, 3, 2).reshape(G, C * P, N)
    wsr_re = lb_re[1][:, :C].transpose(0, 1, 3, 2).reshape(G, C * P, N)
    wsr_im = lb_im[1][:, :C].transpose(0, 1, 3, 2).reshape(G, C * P, N)
    def state_out(d, powers_re, powers_im):
        cr, ci = c_re[d][:, None], c_im[d][:, None]
        pr, pi_ = powers_re[:, :, None], powers_im[:, :, None]
        from_re = (cr * pr - ci * pi_).transpose(0, 3, 1, 2).reshape(G, N, C * P)
        from_im = (-(cr * pi_ + ci * pr)).transpose(0, 3, 1, 2).reshape(G, N, C * P)
        return from_re, from_im
    of_re, of_im = state_out(0, lp_re[0][:, 1:C + 1], lp_im[0][:, 1:C + 1])
    or_re, or_im = state_out(1, lp_re[1][:, C:0:-1], lp_im[1][:, C:0:-1])
    w_state = jnp.concatenate([_pair_blockdiag(w) for w in (wsf_re, wsf_im, wsr_re, wsr_im)], axis=-1)
    w_out = jnp.concatenate([_pair_blockdiag(w) for w in (m, of_re, of_im, or_re, or_im)], axis=1)
    lam_c = jnp.stack([lp_re[0][:, C], lp_im[0][:, C], lp_re[1][:, C], lp_im[1][:, C]])
    return w_state.astype(BF16), w_out.astype(BF16), lam_c.reshape(4, 1, S5_LANES)


def _s5_state_kernel(u_ref, w_ref, fre_ref, fim_ref, rre_ref, rim_ref):
    r = jnp.dot(u_ref[...], w_ref[...], preferred_element_type=F32)
    for k, ref in enumerate((fre_ref, fim_ref, rre_ref, rim_ref)):
        ref[...] = r[:, k * LANES:(k + 1) * LANES]


def _s5_local_states(u2, w_state, l):
    uw = 2 * S5_CHUNK * S5_GROUP_CH
    out = jax.ShapeDtypeStruct((S5_ROWS, S5_LANES), F32)
    ospec = pl.BlockSpec((S5_ROWS, LANES), lambda q: (0, q))
    return pl.pallas_call(
        _s5_state_kernel,
        grid=(S5_UNITS,),
        in_specs=[pl.BlockSpec((None, S5_ROWS, uw), lambda q: (q, 0, 0)),
                  pl.BlockSpec((None, None, uw, 4 * LANES), lambda q: (l, q, 0, 0))],
        out_specs=[ospec] * 4,
        out_shape=[out] * 4,
        compiler_params=_cparams("parallel"),
    )(u2, w_state)


S5_SCAN_LANES = 512
S5_PAIR_ROWS = 2 * BATCH


def _s5_scan_kernel(fre_ref, fim_ref, rre_ref, rim_ref, lam_ref,
                    pfre_ref, pfim_ref, prre_ref, prim_ref):
    lam = lam_ref[...]
    shape = (S5_PAIR_ROWS, S5_SCAN_LANES)
    top = lax.broadcasted_iota(I32, shape, 0) < BATCH
    n_pairs = S5_NCHUNK // 2
    ctx_pairs = S5_CTX_CHUNKS // 2

    def cmul_add(ar, ai, sr, si, lr, li):
        return ar * sr - ai * si + lr, ar * si + ai * sr + li

    def half_step(ar, ai, sr, si, lr, li, first_top):
        xr, xi = cmul_add(ar, ai, sr, si, lr, li)
        xr, xi = pltpu.roll(xr, BATCH, 0), pltpu.roll(xi, BATCH, 0)
        keep = top if first_top else ~top
        prev_r, prev_i = jnp.where(keep, sr, xr), jnp.where(keep, si, xi)
        yr, yi = cmul_add(ar, ai, xr, xi, lr, li)
        nr = jnp.where(keep, pltpu.roll(yr, BATCH, 0), yr)
        ni = jnp.where(keep, pltpu.roll(yi, BATCH, 0), yi)
        return prev_r, prev_i, nr, ni

    def body(kk, carry):
        sfr, sfi, srr, sri = carry
        fo = pl.multiple_of(kk * S5_PAIR_ROWS, S5_PAIR_ROWS)
        rp = jnp.where(kk < ctx_pairs, ctx_pairs - 1 - kk, n_pairs - 1 + ctx_pairs - kk)
        ro = pl.multiple_of(rp * S5_PAIR_ROWS, S5_PAIR_ROWS)
        pr, pi_, sfr, sfi = half_step(lam[0], lam[1], sfr, sfi,
                                      fre_ref[pl.ds(fo, S5_PAIR_ROWS), :],
                                      fim_ref[pl.ds(fo, S5_PAIR_ROWS), :], True)
        pfre_ref[pl.ds(fo, S5_PAIR_ROWS), :] = pr
        pfim_ref[pl.ds(fo, S5_PAIR_ROWS), :] = pi_
        pr, pi_, srr, sri = half_step(lam[2], lam[3], srr, sri,
                                      rre_ref[pl.ds(ro, S5_PAIR_ROWS), :],
                                      rim_ref[pl.ds(ro, S5_PAIR_ROWS), :], False)
        prre_ref[pl.ds(ro, S5_PAIR_ROWS), :] = pr
        prim_ref[pl.ds(ro, S5_PAIR_ROWS), :] = pi_
        return sfr, sfi, srr, sri

    zero = jnp.zeros(shape, F32)
    lax.fori_loop(0, n_pairs, body, (zero, zero, zero, zero))


def _s5_chunk_scan(loc, lam_c, l):
    spec = pl.BlockSpec((S5_ROWS, S5_SCAN_LANES), lambda j: (0, j))
    out = jax.ShapeDtypeStruct((S5_ROWS, S5_LANES), F32)
    return pl.pallas_call(
        _s5_scan_kernel,
        grid=(S5_LANES // S5_SCAN_LANES,),
        in_specs=[spec] * 4 + [pl.BlockSpec((None, 4, 1, S5_SCAN_LANES), lambda j: (l, 0, 0, j))],
        out_specs=[spec] * 4,
        out_shape=[out] * 4,
        compiler_params=_cparams("parallel"),
    )(*loc, lam_c)


def _s5_out_kernel(u_ref, fre_ref, fim_ref, rre_ref, rim_ref, w_ref, o_ref):
    lhs = jnp.concatenate([u_ref[...]] + [r[...].astype(BF16)
                                           for r in (fre_ref, fim_ref, rre_ref, rim_ref)], axis=1)
    o_ref[...] = jnp.dot(lhs, w_ref[...], preferred_element_type=F32).astype(o_ref.dtype)


def _s5_outputs(u2, prev, w_out, l):
    uw = 2 * S5_CHUNK * S5_GROUP_CH
    sspec = pl.BlockSpec((S5_ROWS, LANES), lambda q: (0, q))
    return pl.pallas_call(
        _s5_out_kernel,
        grid=(S5_UNITS,),
        in_specs=[pl.BlockSpec((None, S5_ROWS, uw), lambda q: (q, 0, 0))] + [sspec] * 4
                 + [pl.BlockSpec((None, None, uw + 4 * LANES, uw), lambda q: (l, q, 0, 0))],
        out_specs=pl.BlockSpec((None, S5_ROWS, uw), lambda q: (q, 0, 0)),
        out_shape=jax.ShapeDtypeStruct((S5_UNITS, S5_ROWS, uw), BF16),
        compiler_params=_cparams("parallel"),
    )(u2, *prev, w_out)


def _glu_kernel(y_ref, w_ref, o_ref):
    y = y_ref[...].astype(F32)
    g = 0.5 * y * (1.0 + jnp.tanh(math.sqrt(2.0 / math.pi) * (y + 0.044715 * (y * y * y))))
    t = jnp.dot(g.astype(BF16), w_ref[...].astype(BF16), preferred_element_type=F32)
    o_ref[...] = (g * _sigmoid(t)).astype(o_ref.dtype)


def _s5_glu(y, w_glu, l):
    rows = y.shape[0]
    tm = 512
    return pl.pallas_call(
        _glu_kernel,
        grid=(rows // tm,),
        in_specs=[pl.BlockSpec((tm, S5_WIDTH), lambda i: (i, 0)),
                  pl.BlockSpec((None, S5_WIDTH, S5_WIDTH), lambda i: (l, 0, 0))],
        out_specs=pl.BlockSpec((tm, S5_WIDTH), lambda i: (i, 0)),
        out_shape=jax.ShapeDtypeStruct((rows, S5_WIDTH), BF16),
        compiler_params=_cparams("parallel"),
    )(y, w_glu)


def _s5_branch(z, s5w, w_glu, l):
    w_state, w_out, lam_c = s5w
    C, P = S5_CHUNK, S5_GROUP_CH
    u = lax.optimization_barrier(z[:, COL_U:COL_U + S5_WIDTH])
    u_seq = jnp.concatenate([u[N_LAT:].reshape(BATCH, CTX_LEN, S5_WIDTH),
                             u[:N_LAT].reshape(BATCH, SEQ, S5_WIDTH)], axis=1)
    u2 = (u_seq.astype(BF16).reshape(BATCH, S5_NCHUNK, C, S5_UNITS, 2, P)
          .transpose(3, 1, 0, 4, 2, 5).reshape(S5_UNITS, S5_ROWS, 2 * C * P))
    loc = _s5_local_states(u2, w_state, l)
    prev = _s5_chunk_scan(loc, lam_c, l)
    y2 = _s5_outputs(u2, prev, w_out, l)
    y = (y2.reshape(S5_UNITS, S5_NCHUNK, BATCH, 2, C, P)
         .transpose(2, 1, 4, 0, 3, 5).reshape(BATCH, S5_SEQ, S5_WIDTH))
    y_all = jnp.concatenate([y[:, CTX_LEN:].reshape(N_LAT, S5_WIDTH),
                             y[:, :CTX_LEN].reshape(N_CTX, S5_WIDTH)], axis=0)
    return _s5_glu(y_all, w_glu, l)


def _merge_kernel(ya_ref, ys_ref, yc_ref, wa_ref, ws_ref, wc_ref, ga_ref, gs_ref, gc_ref, o_ref):
    def branch(y_ref, w_ref, g_ref):
        return _sigmoid(g_ref[...]) * jnp.dot(y_ref[...], w_ref[...].astype(BF16),
                                              preferred_element_type=F32)
    o_ref[...] = (branch(ya_ref, wa_ref, ga_ref) + branch(ys_ref, ws_ref, gs_ref)
                  + branch(yc_ref, wc_ref, gc_ref)).astype(o_ref.dtype)


def _gated_merge(ya, ys, yc, z, w_br_a, w_br_s, w_br_c, l):
    rows = ya.shape[0]
    tm, tn = 1024, 512
    aspec = pl.BlockSpec((tm, NA_WIDTH), lambda i, j: (i, 0))
    wspec = pl.BlockSpec((None, NA_WIDTH, tn), lambda i, j: (l, 0, j))
    gspec = lambda col: pl.BlockSpec((tm, tn), lambda i, j: (i, col // tn + j))
    return pl.pallas_call(
        _merge_kernel,
        grid=(rows // tm, D_MODEL // tn),
        in_specs=[aspec] * 3 + [wspec] * 3 + [gspec(COL_GA), gspec(COL_GS), gspec(COL_GC)],
        out_specs=pl.BlockSpec((tm, tn), lambda i, j: (i, j)),
        out_shape=jax.ShapeDtypeStruct((rows, D_MODEL), BF16),
        compiler_params=_cparams("parallel", "parallel"),
    )(ya, ys, yc, w_br_a, w_br_s, w_br_c, z, z, z)


def _proj_res_kernel(m_ref, w_ref, x_ref, g_ref, o_ref):
    y = jnp.dot(m_ref[...], w_ref[...].astype(BF16), preferred_element_type=F32)
    o_ref[...] = x_ref[...] + g_ref[...] * y


def _proj_residual(m, w_out, l, x, mod3, gate_idx):
    rows = m.shape[0]
    tm, tn = 1024, 512
    nj = D_MODEL // tn
    return pl.pallas_call(
        _proj_res_kernel,
        grid=(rows // tm, nj),
        in_specs=[
            pl.BlockSpec((tm, D_MODEL), lambda i, j: (i, 0)),
            pl.BlockSpec((None, D_MODEL, tn), lambda i, j: (l, 0, j)),
            pl.BlockSpec((tm, tn), lambda i, j: (i, j)),
            pl.BlockSpec((None, 1, tn), lambda i, j: (_batch_of_block(i, tm), 0, gate_idx * nj + j)),
        ],
        out_specs=pl.BlockSpec((tm, tn), lambda i, j: (i, j)),
        out_shape=jax.ShapeDtypeStruct((rows, D_MODEL), F32),
        compiler_params=_cparams("parallel", "parallel"),
    )(m, w_out, x, mod3)


def _router_kernel(x_ref, g_ref, sc_ref, sh_ref, wr_ref, br_ref, h_ref, idx_ref, wt_ref, rank_ref,
                   cnt_ref, count):
    @pl.when(pl.program_id(0) == 0)
    def _():
        count[...] = jnp.zeros(count.shape, count.dtype)

    x = x_ref[...]
    tm = x.shape[0]
    y = x * lax.rsqrt(jnp.mean(x * x, axis=-1, keepdims=True) + RMS_EPS) * g_ref[...]
    h = y * (1.0 + sc_ref[...]) + sh_ref[...]
    for k in range(ROW_CHUNKS):
        h_ref[pl.ds(k, tm, stride=ROW_CHUNKS), :] = h[:, k * LANES:(k + 1) * LANES]
    logits = jnp.dot(h, wr_ref[...], preferred_element_type=F32,
                     precision=lax.Precision.HIGHEST) + br_ref[...]
    lane = lax.broadcasted_iota(I32, logits.shape, 1)
    vals, idxs = [], []
    for _ in range(TOP_K):
        m = jnp.max(logits, axis=-1, keepdims=True)
        sel = jnp.min(jnp.where(logits == m, lane, LANES), axis=-1, keepdims=True)
        vals.append(m)
        idxs.append(sel)
        logits = jnp.where(lane == sel, NEG, logits)
    es = [jnp.exp(v - vals[0]) for v in vals]
    den = es[0] + es[1] + es[2] + es[3]
    hits = [lane == idxs[k] for k in range(TOP_K)]
    per_expert = hits[0].astype(F32)
    for k in range(1, TOP_K):
        per_expert = per_expert + hits[k].astype(F32)
    tri = (lax.broadcasted_iota(I32, (tm, tm), 1) < lax.broadcasted_iota(I32, (tm, tm), 0)).astype(BF16)
    before = jnp.dot(tri, per_expert.astype(BF16), preferred_element_type=F32) + count[...]
    idx_out = jnp.zeros(lane.shape, I32)
    wt_out = jnp.zeros(lane.shape, F32)
    rank_out = jnp.zeros(lane.shape, I32)
    for k in range(TOP_K):
        rank_k = jnp.sum(jnp.where(hits[k], before, 0.0), axis=-1, keepdims=True).astype(I32)
        idx_out = jnp.where(lane == k, idxs[k], idx_out)
        wt_out = jnp.where(lane == k, es[k] / den, wt_out)
        rank_out = jnp.where(lane == k, rank_k, rank_out)
    idx_ref[...] = idx_out
    wt_ref[...] = wt_out
    rank_ref[...] = rank_out
    count[...] = count[...] + jnp.sum(per_expert, axis=0, keepdims=True)
    cnt_ref[...] = count[...]


def _norm_router(x, g, mod3, shift_idx, scale_idx, w_router, b_router):
    rows = x.shape[0]
    tm = 256
    wr = jnp.zeros((D_MODEL, LANES), F32).at[:, :N_EXPERTS].set(w_router)
    br = jnp.full((1, LANES), NEG, F32).at[0, :N_EXPERTS].set(b_router)
    small = pl.BlockSpec((tm, LANES), lambda i: (i, 0))
    return pl.pallas_call(
        _router_kernel,
        grid=(rows // tm,),
        in_specs=[
            pl.BlockSpec((tm, D_MODEL), lambda i: (i, 0)),
            pl.BlockSpec((1, D_MODEL), lambda i: (0, 0)),
            pl.BlockSpec((None, 1, D_MODEL), lambda i: (_batch_of_block(i, tm), 0, scale_idx)),
            pl.BlockSpec((None, 1, D_MODEL), lambda i: (_batch_of_block(i, tm), 0, shift_idx)),
            pl.BlockSpec((D_MODEL, LANES), lambda i: (0, 0)),
            pl.BlockSpec((1, LANES), lambda i: (0, 0)),
        ],
        out_specs=[pl.BlockSpec((tm * ROW_CHUNKS, LANES), lambda i: (i, 0)), small, small, small,
                   pl.BlockSpec((1, LANES), lambda i: (0, 0))],
        out_shape=[jax.ShapeDtypeStruct((rows * ROW_CHUNKS, LANES), F32),
                   jax.ShapeDtypeStruct((rows, LANES), I32),
                   jax.ShapeDtypeStruct((rows, LANES), F32),
                   jax.ShapeDtypeStruct((rows, LANES), I32),
                   jax.ShapeDtypeStruct((1, LANES), F32)],
        scratch_shapes=[pltpu.VMEM((1, LANES), F32)],
        compiler_params=_cparams("arbitrary"),
    )(x, g.reshape(1, D_MODEL), mod3, mod3, wr, br)


W2_PREP_ROWS = 256


def _w2_prep_kernel(w_ref, o_ref, scr):
    half = LANES // 2
    for k in range(ROW_CHUNKS):
        cols = slice(k * LANES, (k + 1) * LANES)
        s = scr.at[k]
        for t in range(W2_PREP_ROWS // LANES):
            s[pl.ds(t * LANES, half, stride=2), :] = w_ref[t * LANES:t * LANES + half, cols]
            s[pl.ds(t * LANES + 1, half, stride=2), :] = w_ref[t * LANES + half:(t + 1) * LANES, cols]
        o_ref[:, cols] = s[...].astype(o_ref.dtype)


def _w2_prep(w2):
    de = DEPTH * N_EXPERTS
    spec = pl.BlockSpec((None, W2_PREP_ROWS, D_MODEL), lambda e, r: (e, r, 0))
    out = pl.pallas_call(
        _w2_prep_kernel,
        grid=(de, D_FF_EXPERT // W2_PREP_ROWS),
        in_specs=[spec],
        out_specs=spec,
        out_shape=jax.ShapeDtypeStruct((de, D_FF_EXPERT, D_MODEL), BF16),
        scratch_shapes=[pltpu.VMEM((ROW_CHUNKS, W2_PREP_ROWS, LANES), F32)],
        compiler_params=_cparams("parallel", "parallel"),
    )(w2.reshape(de, D_FF_EXPERT, D_MODEL))
    return out.reshape(DEPTH, N_EXPERTS, D_FF_EXPERT, D_MODEL)


def _expert_kernel(be_ref, tok_ref, nused_ref, h_hbm, w1_ref, b1_ref, w2_ref, b2_ref,
                   o_ref, gbuf, xb, sem):
    i = pl.program_id(0)
    n_used = nused_ref[0]
    slot = i % 2

    def start_gather(blk, s):
        base = blk * MOE_BLOCK

        def issue(j, c):
            pltpu.make_async_copy(h_hbm.at[tok_ref[base + j]],
                                  gbuf.at[s, pl.ds(j * ROW_CHUNKS, ROW_CHUNKS), :], sem.at[s]).start()
            return c
        lax.fori_loop(0, MOE_BLOCK, issue, 0, unroll=8)

    def wait_gather(s):
        def wait(j, c):
            pltpu.make_async_copy(h_hbm.at[0], gbuf.at[s, pl.ds(j * ROW_CHUNKS, ROW_CHUNKS), :],
                                  sem.at[s]).wait()
            return c
        lax.fori_loop(0, MOE_BLOCK, wait, 0, unroll=8)

    @pl.when(i == 0)
    def _():
        start_gather(0, 0)

    @pl.when(i + 1 < n_used)
    def _():
        start_gather(i + 1, 1 - slot)

    @pl.when(i < n_used)
    def _():
        wait_gather(slot)
        g = gbuf.at[slot]
        for k in range(ROW_CHUNKS):
            xb[:, k * LANES:(k + 1) * LANES] = g[pl.ds(k, MOE_BLOCK, stride=ROW_CHUNKS), :].astype(BF16)
        a = jnp.dot(xb[...], w1_ref[...], preferred_element_type=F32) + b1_ref[...]
        even = lax.broadcasted_iota(I32, (MOE_BLOCK, LANES), 1) % 2 == 0
        prods = []
        for s in range(2 * D_FF_EXPERT // LANES):
            a_s = a[:, s * LANES:(s + 1) * LANES]
            gate = jnp.minimum(a_s, SWIGLU_LIMIT)
            lin = jnp.clip(a_s, -SWIGLU_LIMIT, SWIGLU_LIMIT) + 1.0
            prods.append(gate * _sigmoid(SWIGLU_ALPHA * gate) * pltpu.roll(lin, LANES - 1, 1))
        acts = [jnp.where(even, prods[2 * t], pltpu.roll(prods[2 * t + 1], 1, 1))
                for t in range(D_FF_EXPERT // LANES)]
        act = jnp.concatenate(acts, axis=1).astype(BF16)
        y = jnp.dot(act, w2_ref[...], preferred_element_type=F32) + b2_ref[...]
        for k in range(ROW_CHUNKS):
            o_ref[pl.ds(k, MOE_BLOCK, stride=ROW_CHUNKS), :] = y[:, k * LANES:(k + 1) * LANES]

    @pl.when(i >= n_used)
    def _():
        o_ref[...] = jnp.zeros(o_ref.shape, o_ref.dtype)


def _expert_ffn(h_rows, block_expert, slot_token, n_used, w1, b1, w2, b2, l):
    nb = block_expert.shape[0]
    rows = h_rows.shape[0] // ROW_CHUNKS
    grid_spec = pltpu.PrefetchScalarGridSpec(
        num_scalar_prefetch=3,
        grid=(nb,),
        in_specs=[
            pl.BlockSpec(memory_space=pl.ANY),
            pl.BlockSpec((None, None, D_MODEL, 2 * D_FF_EXPERT), lambda i, be, tok, nu: (l, be[i], 0, 0)),
            pl.BlockSpec((None, None, 1, 2 * D_FF_EXPERT), lambda i, be, tok, nu: (l, be[i], 0, 0)),
            pl.BlockSpec((None, None, D_FF_EXPERT, D_MODEL), lambda i, be, tok, nu: (l, be[i], 0, 0)),
            pl.BlockSpec((None, None, 1, D_MODEL), lambda i, be, tok, nu: (l, be[i], 0, 0)),
        ],
        out_specs=pl.BlockSpec((MOE_BLOCK * ROW_CHUNKS, LANES), lambda i, be, tok, nu: (i, 0)),
        scratch_shapes=[
            pltpu.VMEM((2, MOE_BLOCK * ROW_CHUNKS, LANES), F32),
            pltpu.VMEM((MOE_BLOCK, D_MODEL), BF16),
            pltpu.SemaphoreType.DMA((2,)),
        ],
    )
    return pl.pallas_call(
        _expert_kernel,
        grid_spec=grid_spec,
        out_shape=jax.ShapeDtypeStruct((nb * MOE_BLOCK * ROW_CHUNKS, LANES), F32),
        compiler_params=_cparams("arbitrary"),
    )(block_expert, slot_token, n_used, h_rows.reshape(rows, ROW_CHUNKS, LANES), w1, b1, w2, b2)


COMB_TM = 128


def _combine_kernel(src_ref, y_hbm, x_ref, g_ref, wt_ref, o_ref, cbuf, sem):
    i = pl.program_id(0)
    n = pl.num_programs(0)
    slot = i % 2
    n_rows = COMB_TM * TOP_K

    def start_gather(blk, s):
        base = blk * n_rows

        def issue(p, c):
            for prio in range(2):
                r = 2 * p + prio
                pltpu.make_async_copy(y_hbm.at[src_ref[base + r]],
                                      cbuf.at[s, pl.ds(r * ROW_CHUNKS, ROW_CHUNKS), :],
                                      sem.at[s]).start(priority=prio)
            return c
        lax.fori_loop(0, n_rows // 2, issue, 0, unroll=4)

    @pl.when(i == 0)
    def _():
        start_gather(0, 0)

    @pl.when(i + 1 < n)
    def _():
        start_gather(i + 1, 1 - slot)

    def wait(r, c):
        pltpu.make_async_copy(y_hbm.at[0], cbuf.at[slot, pl.ds(r * ROW_CHUNKS, ROW_CHUNKS), :],
                              sem.at[slot]).wait()
        return c
    lax.fori_loop(0, n_rows, wait, 0, unroll=8)
    c = cbuf.at[slot]
    wt = wt_ref[...]
    wk = [jnp.broadcast_to(wt[:, kk:kk + 1], (COMB_TM, LANES)) for kk in range(TOP_K)]
    for k in range(ROW_CHUNKS):
        f = wk[0] * c[pl.ds(k, COMB_TM, stride=ROW_CHUNKS), :]
        for kk in range(1, TOP_K):
            f = f + wk[kk] * c[pl.ds(kk * COMB_TM * ROW_CHUNKS + k, COMB_TM, stride=ROW_CHUNKS), :]
        sl = slice(k * LANES, (k + 1) * LANES)
        o_ref[:, sl] = x_ref[:, sl] + g_ref[:, sl] * f


def _moe_combine(src, y_rows, wt_pad, x, mod3, gate_idx):
    rows = x.shape[0]
    n_slots = y_rows.shape[0] // ROW_CHUNKS
    grid_spec = pltpu.PrefetchScalarGridSpec(
        num_scalar_prefetch=1,
        grid=(rows // COMB_TM,),
        in_specs=[
            pl.BlockSpec(memory_space=pl.ANY),
            pl.BlockSpec((COMB_TM, D_MODEL), lambda i, d: (i, 0)),
            pl.BlockSpec((None, 1, D_MODEL), lambda i, d: (_batch_of_block(i, COMB_TM), 0, gate_idx)),
            pl.BlockSpec((COMB_TM, LANES), lambda i, d: (i, 0)),
        ],
        out_specs=pl.BlockSpec((COMB_TM, D_MODEL), lambda i, d: (i, 0)),
        scratch_shapes=[
            pltpu.VMEM((2, COMB_TM * TOP_K * ROW_CHUNKS, LANES), F32),
            pltpu.SemaphoreType.DMA((2,)),
        ],
    )
    return pl.pallas_call(
        _combine_kernel,
        grid_spec=grid_spec,
        out_shape=jax.ShapeDtypeStruct((rows, D_MODEL), F32),
        compiler_params=_cparams("arbitrary"),
    )(src, y_rows.reshape(n_slots, ROW_CHUNKS, LANES), x, mod3, wt_pad)


def _moe_ffn(x, g, mod3, w_router, b_router, w1, b1, w2, b2, l):
    rows = x.shape[0]
    n_assign = rows * TOP_K
    nb = n_assign // MOE_BLOCK + N_EXPERTS
    h_rows, idx_pad, wt_pad, rank_pad, cnt = _norm_router(x, g, mod3, 3, 4, w_router, b_router)
    experts = jnp.arange(N_EXPERTS, dtype=I32)
    counts = cnt[0, :N_EXPERTS].astype(I32)
    padded = (counts + MOE_BLOCK - 1) // MOE_BLOCK * MOE_BLOCK
    padded_end = jnp.cumsum(padded)
    padded_start = padded_end - padded
    idx = idx_pad[:, :TOP_K]
    start_of = jnp.sum(jnp.where(idx[..., None] == experts, padded_start, 0), axis=-1)
    dest = start_of + rank_pad[:, :TOP_K]
    block_start = jnp.arange(nb, dtype=I32) * MOE_BLOCK
    block_expert = jnp.minimum(jnp.sum((block_start[:, None] >= padded_end[None, :]).astype(I32), axis=1),
                               N_EXPERTS - 1)
    n_used = padded_end[-1:] // MOE_BLOCK
    slot_token = jnp.zeros((nb * MOE_BLOCK,), I32).at[dest.reshape(-1)].set(
        jnp.arange(n_assign, dtype=I32) // TOP_K)
    y_rows = _expert_ffn(h_rows, block_expert, slot_token, n_used, w1, b1, w2, b2, l)
    src = dest.reshape(rows // COMB_TM, COMB_TM, TOP_K).transpose(0, 2, 1).reshape(-1)
    return _moe_combine(src, y_rows, wt_pad, x, mod3, 5)


def kernel(x, c, ctx, c_ctx, norm1_g, norm2_g, w_ada, b_ada, w_in, na_rel_bias, q_norm_g, k_norm_g,
           s5_lam_re, s5_lam_im, s5_log_dt, s5_b_re, s5_b_im, s5_c_re, s5_c_im, s5_d, s5_w_glu,
           w_br_a, w_br_s, w_br_c, w_out, w_router, b_router, w_exp1, b_exp1, w_exp2, b_exp2,
           final_norm_g):
    cc = jnp.concatenate([c, c_ctx[None], jnp.zeros((8 - BATCH - 1, D_MODEL), F32)], axis=0)
    mod = _ada_mod(cc, w_ada, b_ada)
    cos, sin = _rope_tables()
    per_layer = [_s5_weights(s5_lam_re[l], s5_lam_im[l], s5_log_dt[l], s5_b_re[l], s5_b_im[l],
                             s5_c_re[l], s5_c_im[l], s5_d[l]) for l in range(DEPTH)]
    s5w = tuple(jnp.stack(ws) for ws in zip(*per_layer))
    w1b = w_exp1.astype(BF16)
    w2b = _w2_prep(w_exp2)
    b1r = b_exp1.reshape(DEPTH, N_EXPERTS, 1, 2 * D_FF_EXPERT)
    b2r = b_exp2.reshape(DEPTH, N_EXPERTS, 1, D_MODEL)
    xa = jnp.concatenate([x.reshape(N_LAT, D_MODEL), ctx.reshape(N_CTX, D_MODEL)], axis=0)
    for l in range(DEPTH):
        ctx_out = l < DEPTH - 1
        mod3 = mod[l].reshape(8, 1, 6 * D_MODEL)
        h = _norm_mod(xa, norm1_g[l], mod3, 0, 1)
        z = _matmul(h, w_in, l, F32)
        pt = _na_bias_table(na_rel_bias[l])
        ya = _na_attention(z, pt)
        qk = _qk_prep(z, q_norm_g[l], k_norm_g[l], cos, sin)
        yc = _gqa_attention(qk, z)
        ys = _s5_branch(z, s5w, s5_w_glu, l)
        if ctx_out:
            ya_c = _ctx_attention(z, COL_QA // HEAD_DIM, z, COL_KA // HEAD_DIM, z, COL_VA // HEAD_DIM,
                                  NA_HEADS, 1)
            yc_c = _ctx_attention(qk, 0, qk, GQA_Q_WIDTH // HEAD_DIM, z, COL_VG // HEAD_DIM,
                                  GQA_KV_HEADS, GQA_GROUP)
            ya = jnp.concatenate([ya, ya_c], axis=0)
            yc = jnp.concatenate([yc, yc_c], axis=0)
        m = _gated_merge(ya, ys, yc, z, w_br_a, w_br_s, w_br_c, l)
        xa = _proj_residual(m, w_out, l, xa, mod3, 2)
        xa = _moe_ffn(xa, norm2_g[l], mod3, w_router[l], b_router[l], w1b, b1r, w2b, b2r, l)
    return _final_norm(xa[:N_LAT], final_norm_g).reshape(BATCH, SEQ, D_MODEL)
```
